```python
import jax, jax.numpy as jnp
from jax import lax
import numpy as np

D_MODEL = 2048
BATCH = 4
SEQ = 4096
DEPTH = 4
DEC_BATCH = 16
DEC_SEQ = 64
PAST_LEN = 1024

CHUNK = 64
N_A_LAYERS = DEPTH // 2
N_B_LAYERS = DEPTH - N_A_LAYERS
GLA_HEADS = 4
GLA_DK = D_MODEL // 2 // GLA_HEADS
GLA_DV = D_MODEL // GLA_HEADS
GLA_RANK = 16
GLA_TAU = 16.0
GLA_QK = GLA_HEADS * GLA_DK
GLA_VR = GLA_HEADS * GLA_DV
GLA_IN = 2 * GLA_QK + 2 * GLA_VR + GLA_RANK
ATT_HEADS = 16
ATT_KV_HEADS = 4
ATT_DH = D_MODEL // ATT_HEADS
ATT_GROUP = ATT_HEADS // ATT_KV_HEADS
LEFT_CHUNKS = 8
ATT_WINDOW = LEFT_CHUNKS * CHUNK
ATT_BAND = ATT_WINDOW + CHUNK
MAX_REL = 256
N_REL = 2 * MAX_REL + 1
D_FF = 5632
CONV_W = 3
EPS = 1e-6

kernel_name = "yoco_gla_chunked_relbias_convffn_step"


def _rms_norm(x, g):
    xf = x.astype(jnp.float32)
    y = xf * lax.rsqrt(jnp.mean(xf * xf, axis=-1, keepdims=True) + EPS)
    return (y * g.astype(jnp.float32)).astype(x.dtype)


def _gla_scan(q, k, v, gk, s0):
    B, T, H, DK = q.shape
    DV = v.shape[-1]
    n = -(-T // CHUNK)
    pad = n * CHUNK - T

    def blocks(a):
        a = jnp.pad(a.astype(jnp.float32), ((0, 0), (0, pad), (0, 0), (0, 0)))
        return jnp.moveaxis(a.reshape(B, n, CHUNK, H, a.shape[-1]), 1, 0)

    tri = jnp.tril(jnp.ones((CHUNK, CHUNK), bool))[None, :, :, None, None]

    def step(S, blk):
        qc, kc, vc, gc = blk
        b = jnp.cumsum(gc, axis=1)
        decay = jnp.exp(jnp.where(tri, b[:, :, None] - b[:, None, :], -jnp.inf))
        att = jnp.einsum('bthk,bshk,btshk->bhts', qc, kc, decay)
        o = (jnp.einsum('bhts,bshv->bthv', att, vc)
             + jnp.einsum('bthk,bhkv->bthv', qc * jnp.exp(b), S))
        b_end = b[:, -1]
        S = (jnp.exp(b_end)[..., None] * S
             + jnp.einsum('bshk,bshv->bhkv', kc * jnp.exp(b_end[:, None] - b), vc))
        return S, o

    S, o = lax.scan(step, s0.astype(jnp.float32), (blocks(q), blocks(k), blocks(v), blocks(gk)))
    o = jnp.moveaxis(o, 0, 1).reshape(B, n * CHUNK, H, DV)[:, :T]
    return o.astype(v.dtype), S.astype(s0.dtype)


def _gla_mixer(h, s0, w_in, w_gate, b_gate, head_norm, w_o):
    B, T, _ = h.shape
    proj = h @ w_in
    q, k, v, r, g_low = jnp.split(
        proj, [GLA_QK, 2 * GLA_QK, 2 * GLA_QK + GLA_VR, 2 * GLA_QK + 2 * GLA_VR], axis=-1)
    q = q.reshape(B, T, GLA_HEADS, GLA_DK) * (GLA_DK ** -0.5)
    k = k.reshape(B, T, GLA_HEADS, GLA_DK)
    v = v.reshape(B, T, GLA_HEADS, GLA_DV)
    gk = jax.nn.log_sigmoid((g_low @ w_gate + b_gate).astype(jnp.float32)) / GLA_TAU
    gk = gk.reshape(B, T, GLA_HEADS, GLA_DK)
    o, s_new = _gla_scan(q, k, v, gk, s0)
    o = _rms_norm(o, head_norm).reshape(B, T, GLA_VR) * jax.nn.silu(r)
    return o @ w_o, s_new


def _attend(q, k, v, q_pos, k_pos, k_valid, rel_bias):
    B, Tq = q.shape[:2]
    Tk = k.shape[1]
    qg = q.reshape(B, Tq, ATT_KV_HEADS, ATT_GROUP, ATT_DH)
    s = jnp.einsum('bqngd,bknd->bngqk', qg, k).astype(jnp.float32) * (ATT_DH ** -0.5)
    rel = jnp.clip(q_pos[:, None] - k_pos[None, :], -MAX_REL, MAX_REL) + MAX_REL
    bias = rel_bias[:, rel].astype(jnp.float32).reshape(ATT_KV_HEADS, ATT_GROUP, Tq, Tk)
    s = jnp.where(k_valid[None, None, None, None, :], s + bias[None], -jnp.inf)
    p = jax.nn.softmax(s, axis=-1)
    o = jnp.einsum('bngqk,bknd->bqngd', p.astype(v.dtype), v)
    return o.reshape(B, Tq, ATT_HEADS, ATT_DH)


def _band_attention_prompt(q, k, v, rel_bias):
    B, S = q.shape[:2]
    n = S // CHUNK
    kp = jnp.pad(k, ((0, 0), (ATT_WINDOW, 0), (0, 0), (0, 0)))
    vp = jnp.pad(v, ((0, 0), (ATT_WINDOW, 0), (0, 0), (0, 0)))
    qb = jnp.moveaxis(q.reshape(B, n, CHUNK, ATT_HEADS, ATT_DH), 1, 0)

    def one(args):
        c, qc = args
        start = c * CHUNK
        kb = lax.dynamic_slice_in_dim(kp, start, ATT_BAND, axis=1)
        vb = lax.dynamic_slice_in_dim(vp, start, ATT_BAND, axis=1)
        q_pos = start + jnp.arange(CHUNK)
        k_pos = start - ATT_WINDOW + jnp.arange(ATT_BAND)
        return _attend(qc, kb, vb, q_pos, k_pos, k_pos >= 0, rel_bias)

    o = lax.map(one, (jnp.arange(n), qb))
    return jnp.moveaxis(o, 0, 1).reshape(B, S, ATT_HEADS, ATT_DH)


def _band_attention_step(q, k_new, v_new, k_cache, v_cache, rel_bias):
    T = q.shape[1]
    W = k_cache.shape[1]
    k = jnp.concatenate([k_cache.astype(k_new.dtype), k_new], axis=1)
    v = jnp.concatenate([v_cache.astype(v_new.dtype), v_new], axis=1)
    q_pos = PAST_LEN + jnp.arange(T)
    k_pos = jnp.concatenate([PAST_LEN - W + jnp.arange(W), PAST_LEN + jnp.arange(T)])
    return _attend(q, k, v, q_pos, k_pos, k_pos >= 0, rel_bias)


def _conv_ffn(h, prev, w_up, conv_w, conv_b, w_down):
    T = h.shape[1]
    gate, val = jnp.split(h @ w_up, 2, axis=-1)
    ext = jnp.concatenate([prev.astype(gate.dtype), gate], axis=1)
    conv = conv_b
    for j in range(CONV_W):
        conv = conv + conv_w[j] * ext[:, j:j + T]
    return (jax.nn.silu(conv) * val) @ w_down, ext[:, T:]


def _trunk(x, gla_s0, conv_s0, k_cache, v_cache, norm_gains, gla_w_in, gla_w_gate,
           gla_b_gate, gla_head_norm, gla_w_o, kv_norm, att_w_kv, att_w_q, att_rel_bias,
           att_w_o, ffn_w_up, ffn_conv_w, ffn_conv_b, ffn_w_down):
    prompt = k_cache is None
    B, T, _ = x.shape
    gla_new, conv_new = [], []
    k_sh = v_sh = None
    for l in range(DEPTH):
        g = norm_gains[l]
        if l == N_A_LAYERS:
            kv = _rms_norm(x, kv_norm) @ att_w_kv
            k_sh, v_sh = jnp.split(kv, 2, axis=-1)
            k_sh = k_sh.reshape(B, T, ATT_KV_HEADS, ATT_DH)
            v_sh = v_sh.reshape(B, T, ATT_KV_HEADS, ATT_DH)
        h = _rms_norm(x, g[0])
        if l < N_A_LAYERS:
            m, s = _gla_mixer(h, gla_s0[l], gla_w_in[l], gla_w_gate[l], gla_b_gate[l],
                              gla_head_norm[l], gla_w_o[l])
            gla_new.append(s)
        else:
            j = l - N_A_LAYERS
            q = (h @ att_w_q[j]).reshape(B, T, ATT_HEADS, ATT_DH)
            if prompt:
                o = _band_attention_prompt(q, k_sh, v_sh, att_rel_bias[j])
            else:
                o = _band_attention_step(q, k_sh, v_sh, k_cache, v_cache, att_rel_bias[j])
            m = o.reshape(B, T, ATT_HEADS * ATT_DH) @ att_w_o[j]
        x = x + _rms_norm(m, g[1])
        f, c = _conv_ffn(_rms_norm(x, g[2]), conv_s0[l], ffn_w_up[l], ffn_conv_w[l],
                         ffn_conv_b[l], ffn_w_down[l])
        conv_new.append(c)
        x = x + _rms_norm(f, g[3])
    if prompt:
        keep = min(ATT_WINDOW, T)
        k_rows, v_rows = k_sh[:, T - keep:], v_sh[:, T - keep:]
    else:
        k_rows, v_rows = k_sh, v_sh
    return x, jnp.stack(gla_new), jnp.stack(conv_new), k_rows, v_rows


def setup_inputs(seed: int = 0) -> dict:
    key = jax.random.key(seed)
    ks = jax.random.split(key, 24)
    f32 = jnp.float32

    def nrm(k, shape, scale):
        return jax.random.normal(k, shape, f32) * scale

    kv_win = min(ATT_WINDOW, PAST_LEN)
    return {
        "x_prompt": nrm(ks[0], (BATCH, SEQ, D_MODEL), 1.0),
        "x_sample": nrm(ks[1], (DEC_BATCH, DEC_SEQ, D_MODEL), 1.0),
        "state_gla": nrm(ks[2], (N_A_LAYERS, DEC_BATCH, GLA_HEADS, GLA_DK, GLA_DV), 1.0),
        "state_ffn_conv": nrm(ks[3], (DEPTH, DEC_BATCH, CONV_W - 1, D_FF), 1.0),
        "cache_k": nrm(ks[4], (DEC_BATCH, kv_win, ATT_KV_HEADS, ATT_DH), 1.0),
        "cache_v": nrm(ks[5], (DEC_BATCH, kv_win, ATT_KV_HEADS, ATT_DH), 1.0),
        "norm_gains": 1.0 + nrm(ks[6], (DEPTH, 4, D_MODEL), 0.05),
        "gla_w_in": nrm(ks[7], (N_A_LAYERS, D_MODEL, GLA_IN), D_MODEL ** -0.5),
        "gla_w_gate": nrm(ks[8], (N_A_LAYERS, GLA_RANK, GLA_QK), GLA_RANK ** -0.5),
        "gla_b_gate": nrm(ks[9], (N_A_LAYERS, GLA_QK), 0.1),
        "gla_head_norm": 1.0 + nrm(ks[10], (N_A_LAYERS, GLA_DV), 0.05),
        "gla_w_o": nrm(ks[11], (N_A_LAYERS, GLA_VR, D_MODEL), GLA_VR ** -0.5),
        "kv_norm": 1.0 + nrm(ks[12], (D_MODEL,), 0.05),
        "att_w_kv": nrm(ks[13], (D_MODEL, 2 * ATT_KV_HEADS * ATT_DH), D_MODEL ** -0.5),
        "att_w_q": nrm(ks[14], (N_B_LAYERS, D_MODEL, ATT_HEADS * ATT_DH), D_MODEL ** -0.5),
        "att_rel_bias": nrm(ks[15], (N_B_LAYERS, ATT_HEADS, N_REL), 0.1),
        "att_w_o": nrm(ks[16], (N_B_LAYERS, ATT_HEADS * ATT_DH, D_MODEL), (ATT_HEADS * ATT_DH) ** -0.5),
        "ffn_w_up": nrm(ks[17], (DEPTH, D_MODEL, 2 * D_FF), D_MODEL ** -0.5),
        "ffn_conv_w": nrm(ks[18], (DEPTH, CONV_W, D_FF), CONV_W ** -0.5),
        "ffn_conv_b": nrm(ks[19], (DEPTH, D_FF), 0.02),
        "ffn_w_down": nrm(ks[20], (DEPTH, D_FF, D_MODEL), D_FF ** -0.5),
    }


def reference(x_prompt, x_sample, state_gla, state_ffn_conv, cache_k, cache_v, norm_gains,
              gla_w_in, gla_w_gate, gla_b_gate, gla_head_norm, gla_w_o, kv_norm, att_w_kv,
              att_w_q, att_rel_bias, att_w_o, ffn_w_up, ffn_conv_w, ffn_conv_b, ffn_w_down):
    B = x_prompt.shape[0]
    gla_zero = jnp.zeros((N_A_LAYERS, B, GLA_HEADS, GLA_DK, GLA_DV), x_prompt.dtype)
    conv_zero = jnp.zeros((DEPTH, B, CONV_W - 1, D_FF), x_prompt.dtype)
    y_prompt, gla_p, conv_p, k_p, v_p = _trunk(
        x_prompt, gla_zero, conv_zero, None, None, norm_gains, gla_w_in, gla_w_gate,
        gla_b_gate, gla_head_norm, gla_w_o, kv_norm, att_w_kv, att_w_q, att_rel_bias,
        att_w_o, ffn_w_up, ffn_conv_w, ffn_conv_b, ffn_w_down)
    y_sample, gla_s, conv_s, k_s, v_s = _trunk(
        x_sample, state_gla, state_ffn_conv, cache_k, cache_v, norm_gains, gla_w_in,
        gla_w_gate, gla_b_gate, gla_head_norm, gla_w_o, kv_norm, att_w_kv, att_w_q,
        att_rel_bias, att_w_o, ffn_w_up, ffn_conv_w, ffn_conv_b, ffn_w_down)
    return (y_prompt, y_sample, gla_p, gla_s, conv_p, conv_s, k_p, v_p, k_s, v_s)
```

```python
import functools

import jax
import jax.numpy as jnp
from jax import lax
from jax.experimental import pallas as pl
from jax.experimental.pallas import tpu as pltpu

CHUNK = 64
LEFT_CHUNKS = 8
WINDOW = LEFT_CHUNKS * CHUNK
BAND = WINDOW + CHUNK
LANES = 128
BAND_PAD = ((BAND + LANES - 1) // LANES) * LANES
SUB = 16
GLA_TAU = 16.0
EPS = 1e-6
VMEM_LIMIT = 56 * 1024 * 1024

F32 = jnp.float32
BF16 = jnp.bfloat16


def _params(*sem):
    return pltpu.CompilerParams(dimension_semantics=sem, vmem_limit_bytes=VMEM_LIMIT)


def _rms(x, g):
    ms = jnp.mean(x * x, axis=-1, keepdims=True)
    return x * lax.rsqrt(ms + EPS) * g


def _norm_matmul_body(x_ref, g_ref, w_ref, o_ref, xn_ref):
    @pl.when(pl.program_id(1) == 0)
    def _():
        xn_ref[...] = _rms(x_ref[...], g_ref[...]).astype(BF16)

    o_ref[...] = jnp.dot(xn_ref[...], w_ref[...], preferred_element_type=F32).astype(o_ref.dtype)


def _norm_matmul(x, gain, w, *, tm, tn, out_dtype, name):
    M, D = x.shape
    N = w.shape[1]
    return pl.pallas_call(
        _norm_matmul_body,
        grid=(M // tm, N // tn),
        in_specs=[
            pl.BlockSpec((tm, D), lambda i, j: (i, 0)),
            pl.BlockSpec((1, D), lambda i, j: (0, 0)),
            pl.BlockSpec((D, tn), lambda i, j: (0, j)),
        ],
        out_specs=pl.BlockSpec((tm, tn), lambda i, j: (i, j)),
        out_shape=jax.ShapeDtypeStruct((M, N), out_dtype),
        scratch_shapes=[pltpu.VMEM((tm, D), BF16)],
        compiler_params=_params("arbitrary", "arbitrary"),
        name=name,
    )(x, gain, w)


def _matmul_postnorm_body(a_ref, w_ref, g_ref, x_ref, o_ref):
    m = jnp.dot(a_ref[...], w_ref[...], preferred_element_type=F32)
    o_ref[...] = x_ref[...] + _rms(m, g_ref[...])


def _matmul_postnorm(a, w, gain, x, *, tm, name):
    M, K = a.shape
    D = w.shape[1]
    return pl.pallas_call(
        _matmul_postnorm_body,
        grid=(M // tm,),
        in_specs=[
            pl.BlockSpec((tm, K), lambda i: (i, 0)),
            pl.BlockSpec((K, D), lambda i: (0, 0)),
            pl.BlockSpec((1, D), lambda i: (0, 0)),
            pl.BlockSpec((tm, D), lambda i: (i, 0)),
        ],
        out_specs=pl.BlockSpec((tm, D), lambda i: (i, 0)),
        out_shape=jax.ShapeDtypeStruct((M, D), F32),
        compiler_params=_params("arbitrary"),
        name=name,
    )(a, w, gain, x)


def _conv_rows(gate, prev8, cw, cb):
    row = lax.broadcasted_iota(jnp.int32, gate.shape, 0)
    p_old, p_new = prev8[6:7], prev8[7:8]
    s1 = jnp.where(row == 0, p_new, pltpu.roll(gate, 1, axis=0))
    s2 = jnp.where(row == 0, p_old, jnp.where(row == 1, p_new, pltpu.roll(gate, 2, axis=0)))
    return cb + cw[0:1] * s2 + cw[1:2] * s1 + cw[2:3] * gate


def _conv_ffn_body(x_ref, g2_ref, wg_ref, wv_ref, cw_ref, cb_ref, wd_ref, st_ref, g3_ref,
                   o_ref, tail_ref, xn_ref, acc_ref, carry_ref, *, tm, seg, tiles_per_seq):
    i = pl.program_id(0)
    f = pl.program_id(1)

    @pl.when(f == 0)
    def _():
        xn_ref[...] = _rms(x_ref[...], g2_ref[...]).astype(BF16)

    xn = xn_ref[...]
    gate = jnp.dot(xn, wg_ref[...], preferred_element_type=F32)
    val = jnp.dot(xn, wv_ref[...], preferred_element_type=F32)
    cw = cw_ref[...]
    cb = cb_ref[...]
    if seg >= tm:
        @pl.when(i % tiles_per_seq == 0)
        def _():
            carry_ref[f] = st_ref[0]

        conv = _conv_rows(gate, carry_ref[f], cw, cb)
        carry_ref[f] = gate[tm - 8:]
        tail_ref[0] = gate[tm - 8:]
    else:
        parts = []
        for s in range(tm // seg):
            g_s = gate[s * seg:(s + 1) * seg]
            parts.append(_conv_rows(g_s, st_ref[s], cw, cb))
            tail_ref[s] = g_s[seg - 8:]
        conv = jnp.concatenate(parts, axis=0)
    h = (conv * jax.nn.sigmoid(conv) * val).astype(BF16)
    part = jnp.dot(h, wd_ref[...], preferred_element_type=F32)

    @pl.when(f == 0)
    def _():
        acc_ref[...] = part

    @pl.when(f > 0)
    def _():
        acc_ref[...] += part

    @pl.when(f == pl.num_programs(1) - 1)
    def _():
        o_ref[...] = x_ref[...] + _rms(acc_ref[...], g3_ref[...])


def _conv_ffn(x, g2, w_up, conv_w, conv_b, w_down, state8, g3, *, seq_len, tm, tf, name):
    M, D = x.shape
    F = w_down.shape[0]
    nf = F // tf
    n_seq = M // seq_len
    if seq_len >= tm:
        tiles_per_seq = seq_len // tm
        seq_blk = 1
        seq_idx = lambda i, f: (i // tiles_per_seq, 0, f)
        n_tail = M // tm
    else:
        tiles_per_seq = 1
        seq_blk = tm // seq_len
        seq_idx = lambda i, f: (i, 0, f)
        n_tail = n_seq
    body = functools.partial(_conv_ffn_body, tm=tm, seg=seq_len, tiles_per_seq=tiles_per_seq)
    y, tails = pl.pallas_call(
        body,
        grid=(M // tm, nf),
        in_specs=[
            pl.BlockSpec((tm, D), lambda i, f: (i, 0)),
            pl.BlockSpec((1, D), lambda i, f: (0, 0)),
            pl.BlockSpec((D, tf), lambda i, f: (0, f)),
            pl.BlockSpec((D, tf), lambda i, f: (0, nf + f)),
            pl.BlockSpec((3, tf), lambda i, f: (0, f)),
            pl.BlockSpec((1, tf), lambda i, f: (0, f)),
            pl.BlockSpec((tf, D), lambda i, f: (f, 0)),
            pl.BlockSpec((seq_blk, 8, tf), seq_idx),
            pl.BlockSpec((1, D), lambda i, f: (0, 0)),
        ],
        out_specs=[
            pl.BlockSpec((tm, D), lambda i, f: (i, 0)),
            pl.BlockSpec((seq_blk, 8, tf), lambda i, f: (i, 0, f)),
        ],
        out_shape=[
            jax.ShapeDtypeStruct((M, D), F32),
            jax.ShapeDtypeStruct((n_tail, 8, F), F32),
        ],
        scratch_shapes=[
            pltpu.VMEM((tm, D), BF16),
            pltpu.VMEM((tm, D), F32),
            pltpu.VMEM((nf, 8, tf), F32),
        ],
        compiler_params=_params("arbitrary", "arbitrary"),
        name=name,
    )(x, g2, w_up, w_up, conv_w, conv_b, w_down, state8, g3)
    return y, tails[tiles_per_seq - 1::tiles_per_seq]


def _gla_chunk(q, k, v, r, glow, wg, bg, hn, S, *, dk):
    C = CHUNK
    logits = jnp.dot(glow, wg, preferred_element_type=F32, precision=lax.Precision.HIGHEST) + bg
    gk = (jnp.minimum(logits, 0.0) - jnp.log1p(jnp.exp(-jnp.abs(logits)))) * (1.0 / GLA_TAU)
    r_i = lax.broadcasted_iota(jnp.int32, (C, C), 0)
    c_i = lax.broadcasted_iota(jnp.int32, (C, C), 1)
    tri = (r_i >= c_i).astype(F32)
    b = jnp.dot(tri, gk, preferred_element_type=F32, precision=lax.Precision.HIGHEST)
    b_end = b[C - 1:C]
    qs = q * (dk ** -0.5)

    o = jnp.dot((qs * jnp.exp(b)).astype(BF16), S.astype(BF16), preferred_element_type=F32)

    row = lax.broadcasted_iota(jnp.int32, (C, 1), 0)
    n_sub = C // SUB
    ends = [b[(j + 1) * SUB - 1:(j + 1) * SUB] for j in range(n_sub)]
    own_end = jnp.concatenate([jnp.broadcast_to(e, (SUB, e.shape[1])) for e in ends], axis=0)
    k_til = (k * jnp.exp(own_end - b)).astype(BF16)
    col_blk = c_i // SUB
    A = jnp.zeros((C, C), F32)
    for j in range(n_sub - 1):
        live = row >= (j + 1) * SUB
        q_j = jnp.where(live, qs * jnp.exp(jnp.minimum(b - ends[j], 0.0)), 0.0).astype(BF16)
        a_j = lax.dot_general(q_j, k_til, (((1,), (1,)), ((), ())), preferred_element_type=F32)
        A = A + jnp.where(col_blk == j, a_j, 0.0)

    ones = jnp.ones((dk, C), BF16)
    t_sub = lax.broadcasted_iota(jnp.int32, (SUB, 1), 0)
    blk_col0 = (r_i // SUB) * SUB
    for s in range(SUB):
        parts = []
        for d in range(n_sub):
            lo = d * SUB
            b_s = b[lo + s:lo + s + 1]
            k_s = k[lo + s:lo + s + 1]
            e = jnp.where(t_sub >= s, b[lo:lo + SUB] - b_s, -jnp.inf)
            parts.append(qs[lo:lo + SUB] * jnp.exp(e) * k_s)
        p_s = jnp.concatenate(parts, axis=0).astype(BF16)
        col = jnp.dot(p_s, ones, preferred_element_type=F32)
        A = jnp.where(c_i == blk_col0 + s, col, A)

    o = o + jnp.dot(A.astype(BF16), v, preferred_element_type=F32)

    k_dec_t = jnp.transpose(k * jnp.exp(b_end - b)).astype(BF16)
    decay_col = jnp.exp(jnp.transpose(b)[:, C - 1:C])
    S_new = decay_col * S + jnp.dot(k_dec_t, v, preferred_element_type=F32)

    out = _rms(o, hn) * (r * jax.nn.sigmoid(r))
    return out, S_new


def _gla_scan_body(q_ref, k_ref, v_ref, r_ref, gl_ref, wg_ref, bg_ref, hn_ref, s0_ref,
                   o_ref, sn_ref, S_ref, *, n_chunks, dk):
    t = pl.program_id(2)

    @pl.when(t == 0)
    def _():
        S_ref[...] = s0_ref[0, 0]

    wg = wg_ref[...]
    bg = bg_ref[...]
    hn = hn_ref[...]

    def chunk(c, carry):
        rows = pl.ds(pl.multiple_of(c * CHUNK, CHUNK), CHUNK)
        out, S_new = _gla_chunk(
            q_ref[rows, :].astype(F32), k_ref[rows, :].astype(F32), v_ref[rows, :],
            r_ref[rows, :].astype(F32), gl_ref[rows, :], wg, bg, hn, S_ref[...], dk=dk)
        S_ref[...] = S_new
        o_ref[rows, :] = out.astype(o_ref.dtype)
        return carry

    lax.fori_loop(0, n_chunks, chunk, 0)

    @pl.when(t == pl.num_programs(2) - 1)
    def _():
        sn_ref[0, 0] = S_ref[...]


def _gla_scan(proj, glow, w_gate, b_gate, head_norm, s0, *, seq_len, tb, name):
    B, H, DK, DV = s0.shape
    M = proj.shape[0]
    nt = seq_len // tb
    v_blk0 = (2 * H * DK) // DV
    body = functools.partial(_gla_scan_body, n_chunks=tb // CHUNK, dk=DK)
    return pl.pallas_call(
        body,
        grid=(B, H, nt),
        in_specs=[
            pl.BlockSpec((tb, DK), lambda b, h, t: (b * nt + t, h)),
            pl.BlockSpec((tb, DK), lambda b, h, t: (b * nt + t, H + h)),
            pl.BlockSpec((tb, DV), lambda b, h, t: (b * nt + t, v_blk0 + h)),
            pl.BlockSpec((tb, DV), lambda b, h, t: (b * nt + t, v_blk0 + H + h)),
            pl.BlockSpec((tb, LANES), lambda b, h, t: (b * nt + t, 0)),
            pl.BlockSpec((LANES, DK), lambda b, h, t: (0, h)),
            pl.BlockSpec((1, DK), lambda b, h, t: (0, h)),
            pl.BlockSpec((1, DV), lambda b, h, t: (0, 0)),
            pl.BlockSpec((1, 1, DK, DV), lambda b, h, t: (b, h, 0, 0)),
        ],
        out_specs=[
            pl.BlockSpec((tb, DV), lambda b, h, t: (b * nt + t, h)),
            pl.BlockSpec((1, 1, DK, DV), lambda b, h, t: (b, h, 0, 0)),
        ],
        out_shape=[
            jax.ShapeDtypeStruct((M, H * DV), BF16),
            jax.ShapeDtypeStruct(s0.shape, F32),
        ],
        scratch_shapes=[pltpu.VMEM((DK, DV), F32)],
        compiler_params=_params("arbitrary", "arbitrary", "arbitrary"),
        name=name,
    )(proj, proj, proj, proj, glow, w_gate, b_gate, head_norm, s0)


def _band_attention_body(q_ref, k_ref, v_ref, bias_ref, o_ref, *, n_chunks, group, dh, chunk0):
    t = pl.program_id(2)
    rows_q = group * CHUNK
    col = lax.broadcasted_iota(jnp.int32, (rows_q, BAND_PAD), 1)
    bias = bias_ref[0]
    scale = dh ** -0.5

    def chunk(c, carry):
        r0 = pl.multiple_of(c * CHUNK, CHUNK)
        cg = chunk0 + t * n_chunks + c
        start = pl.multiple_of((t * n_chunks + c) * CHUNK, CHUNK)
        qc = q_ref[pl.ds(r0, CHUNK), :]
        q4 = jnp.concatenate([qc[:, g * dh:(g + 1) * dh] for g in range(group)], axis=0)
        kb = k_ref[0, pl.ds(start, BAND_PAD), :]
        vb = v_ref[0, pl.ds(start, BAND_PAD), :]
        s = lax.dot_general(q4, kb, (((1,), (1,)), ((), ())), preferred_element_type=F32)
        s = s * scale + bias
        first_valid = WINDOW - cg * CHUNK
        s = jnp.where(col < BAND, jnp.where(col >= first_valid, s, -jnp.inf), -jnp.inf)
        m = jnp.max(s, axis=-1, keepdims=True)
        p = jnp.exp(s - m)
        l = jnp.sum(p, axis=-1, keepdims=True)
        o4 = jnp.dot(p.astype(BF16), vb, preferred_element_type=F32) / l
        for g in range(group):
            o_ref[pl.ds(r0, CHUNK), g * dh:(g + 1) * dh] = o4[g * CHUNK:(g + 1) * CHUNK].astype(o_ref.dtype)
        return carry

    lax.fori_loop(0, n_chunks, chunk, 0)


def _band_attention(q, kv_band, bias, *, seq_len, tq, chunk0, name):
    M = q.shape[0]
    B, tp, _ = kv_band.shape
    hkv = bias.shape[0]
    group = bias.shape[1] // CHUNK
    dh = q.shape[1] // (hkv * group)
    nt = seq_len // tq
    body = functools.partial(_band_attention_body, n_chunks=tq // CHUNK, group=group, dh=dh, chunk0=chunk0)
    return pl.pallas_call(
        body,
        grid=(B, hkv, nt),
        in_specs=[
            pl.BlockSpec((tq, group * dh), lambda b, n, t: (b * nt + t, n)),
            pl.BlockSpec((1, tp, dh), lambda b, n, t: (b, 0, n)),
            pl.BlockSpec((1, tp, dh), lambda b, n, t: (b, 0, hkv + n)),
            pl.BlockSpec((1, group * CHUNK, BAND_PAD), lambda b, n, t: (n, 0, 0)),
        ],
        out_specs=pl.BlockSpec((tq, group * dh), lambda b, n, t: (b * nt + t, n)),
        out_shape=jax.ShapeDtypeStruct(q.shape, BF16),
        compiler_params=_params("arbitrary", "arbitrary", "arbitrary"),
        name=name,
    )(q, kv_band, kv_band, bias)


def _bias_table(rel_bias, hkv):
    H, n_rel = rel_bias.shape
    max_rel = (n_rel - 1) // 2
    qi = jnp.arange(CHUNK)[:, None]
    kj = jnp.arange(BAND)[None, :] - WINDOW
    rel = jnp.clip(qi - kj, -max_rel, max_rel) + max_rel
    tab = rel_bias[:, rel]
    tab = jnp.pad(tab, ((0, 0), (0, 0), (0, BAND_PAD - BAND)))
    return tab.reshape(hkv, (H // hkv) * CHUNK, BAND_PAD).astype(F32)


def _trunk(x, gla_s0, conv_s0, kv_past, wts, *, seq_len):
    M, D = x.shape
    n_seq = M // seq_len
    tm = min(512, M)
    depth = wts["norm_gains"].shape[0]
    n_a = gla_s0.shape[0]
    hkv, dh = wts["hkv"], wts["dh"]
    gla_new, conv_new = [], []
    kv = None
    for l in range(depth):
        g = wts["norm_gains"][l]
        if l == n_a:
            kv = _norm_matmul(x, wts["kv_norm"], wts["att_w_kv"], tm=tm, tn=1024, out_dtype=F32,
                              name="kv_proj")
            kv3 = kv.reshape(n_seq, seq_len, 2 * hkv * dh).astype(BF16)
            if kv_past is None:
                band = jnp.pad(kv3, ((0, 0), (WINDOW, BAND_PAD - BAND), (0, 0)))
            else:
                band = jnp.pad(jnp.concatenate([kv_past.astype(BF16), kv3], axis=1),
                               ((0, 0), (0, BAND_PAD - BAND), (0, 0)))
        if l < n_a:
            proj = _norm_matmul(x, g[0:1], wts["gla_w_main"][l], tm=tm, tn=1024, out_dtype=BF16,
                                name="gla_in_proj")
            glow = _norm_matmul(x, g[0:1], wts["gla_w_low"][l], tm=tm, tn=LANES, out_dtype=F32,
                                name="gla_gate_proj")
            o, s_new = _gla_scan(proj, glow, wts["gla_w_gate"][l], wts["gla_b_gate"][l],
                                 wts["gla_head_norm"][l], gla_s0[l], seq_len=seq_len,
                                 tb=min(256, seq_len), name="gla_scan")
            gla_new.append(s_new)
            w_o = wts["gla_w_o"][l]
        else:
            j = l - n_a
            q = _norm_matmul(x, g[0:1], wts["att_w_q"][j], tm=tm, tn=1024, out_dtype=BF16, name="att_q_proj")
            o = _band_attention(q, band, wts["att_bias"][j], seq_len=seq_len, tq=min(256, seq_len),
                                chunk0=0 if kv_past is None else LEFT_CHUNKS, name="band_attention")
            w_o = wts["att_w_o"][j]
        x = _matmul_postnorm(o, w_o, g[1:2], x, tm=tm, name="mixer_out_proj")
        x, tail = _conv_ffn(x, g[2:3], wts["ffn_w_up"][l], wts["ffn_conv_w"][l], wts["ffn_conv_b"][l],
                            wts["ffn_w_down"][l], conv_s0[l], g[3:4], seq_len=seq_len, tm=tm, tf=512,
                            name="conv_ffn")
        conv_new.append(tail[:, 6:8])
    return x, jnp.stack(gla_new), jnp.stack(conv_new), kv


def kernel(x_prompt, x_sample, state_gla, state_ffn_conv, cache_k, cache_v, norm_gains, gla_w_in,
           gla_w_gate, gla_b_gate, gla_head_norm, gla_w_o, kv_norm, att_w_kv, att_w_q, att_rel_bias,
           att_w_o, ffn_w_up, ffn_conv_w, ffn_conv_b, ffn_w_down):
    B, S, D = x_prompt.shape
    DB, DS, _ = x_sample.shape
    n_a, _, H, DK, DV = state_gla.shape
    depth = norm_gains.shape[0]
    hkv, dh = cache_k.shape[2], cache_k.shape[3]
    F = ffn_w_down.shape[1]
    assert cache_k.shape[1] == WINDOW and S % CHUNK == 0 and DS == CHUNK
    qkvr = 2 * H * DK + 2 * H * DV
    rank = gla_w_in.shape[2] - qkvr

    wts = dict(
        hkv=hkv, dh=dh,
        norm_gains=norm_gains,
        gla_w_main=gla_w_in[:, :, :qkvr].astype(BF16),
        gla_w_low=jnp.pad(gla_w_in[:, :, qkvr:], ((0, 0), (0, 0), (0, LANES - rank))).astype(BF16),
        gla_w_gate=jnp.pad(gla_w_gate, ((0, 0), (0, LANES - rank), (0, 0))),
        gla_b_gate=gla_b_gate[:, None, :],
        gla_head_norm=gla_head_norm[:, None, :],
        gla_w_o=gla_w_o.astype(BF16),
        kv_norm=kv_norm[None, :],
        att_w_kv=att_w_kv.astype(BF16),
        att_w_q=att_w_q.astype(BF16),
        att_bias=[_bias_table(att_rel_bias[j], hkv) for j in range(depth - n_a)],
        att_w_o=att_w_o.astype(BF16),
        ffn_w_up=ffn_w_up.astype(BF16),
        ffn_conv_w=ffn_conv_w,
        ffn_conv_b=ffn_conv_b[:, None, :],
        ffn_w_down=ffn_w_down.astype(BF16),
    )

    def state8(st):
        return jnp.pad(st, ((0, 0), (0, 0), (6, 0), (0, 0)))

    y_p, gla_p, conv_p, kv_p = _trunk(
        x_prompt.reshape(B * S, D), jnp.zeros((n_a, B, H, DK, DV), F32),
        jnp.zeros((depth, B, 8, F), F32), None, wts, seq_len=S)
    kv_past = jnp.concatenate([cache_k.reshape(DB, WINDOW, hkv * dh),
                               cache_v.reshape(DB, WINDOW, hkv * dh)], axis=-1)
    y_s, gla_s, conv_s, kv_s = _trunk(
        x_sample.reshape(DB * DS, D), state_gla, state8(state_ffn_conv), kv_past, wts, seq_len=DS)

    keep = min(WINDOW, S)
    kv_p = kv_p.reshape(B, S, 2, hkv, dh)[:, S - keep:]
    kv_s = kv_s.reshape(DB, DS, 2, hkv, dh)
    return (y_p.reshape(B, S, D), y_s.reshape(DB, DS, D), gla_p, gla_s, conv_p, conv_s,
            kv_p[:, :, 0], kv_p[:, :, 1], kv_s[:, :, 0], kv_s[:, :, 1])
```

```python
import functools

import numpy as np
import jax
import jax.numpy as jnp
from jax import lax
from jax.experimental import pallas as pl
from jax.experimental.pallas import tpu as pltpu

CHUNK = 64
LEFT_CHUNKS = 8
WINDOW = LEFT_CHUNKS * CHUNK
BAND = WINDOW + CHUNK
LANES = 128
SUBLANES = 8
BAND_PAD = ((BAND + LANES - 1) // LANES) * LANES
GLA_TAU = 16.0
EPS = 1e-6
VMEM_LIMIT = 56 * 1024 * 1024

F32 = jnp.float32
BF16 = jnp.bfloat16


def _params(*sem):
    return pltpu.CompilerParams(dimension_semantics=sem, vmem_limit_bytes=VMEM_LIMIT)


def _rms(x, g):
    ms = jnp.mean(x * x, axis=-1, keepdims=True)
    return x * lax.rsqrt(ms + EPS) * g


def _split_bf16(x, parts):
    out = []
    for _ in range(parts - 1):
        hi = x.astype(BF16)
        out.append(hi)
        x = x - hi.astype(F32)
    out.append(x.astype(BF16))
    return out


def _norm_matmul_body(x_ref, g_ref, w_ref, o_ref, xn_ref):
    @pl.when(pl.program_id(1) == 0)
    def _():
        xn_ref[...] = _rms(x_ref[...], g_ref[...]).astype(BF16)

    o_ref[...] = jnp.dot(xn_ref[...], w_ref[...], preferred_element_type=F32).astype(o_ref.dtype)


def _norm_matmul(x, gains, gain_row, w, layer, *, tm, tn, out_dtype, name):
    M, D = x.shape
    N = w.shape[2]
    return pl.pallas_call(
        _norm_matmul_body,
        grid=(M // tm, N // tn),
        in_specs=[
            pl.BlockSpec((tm, D), lambda i, j: (i, 0)),
            pl.BlockSpec((None, 1, D), lambda i, j: (gain_row, 0, 0)),
            pl.BlockSpec((None, D, tn), lambda i, j: (layer, 0, j)),
        ],
        out_specs=pl.BlockSpec((tm, tn), lambda i, j: (i, j)),
        out_shape=jax.ShapeDtypeStruct((M, N), out_dtype),
        scratch_shapes=[pltpu.VMEM((tm, D), BF16)],
        compiler_params=_params("arbitrary", "arbitrary"),
        name=name,
    )(x, gains, w)


def _matmul_postnorm_body(a_ref, w_ref, g_ref, x_ref, o_ref):
    m = jnp.dot(a_ref[...], w_ref[...], preferred_element_type=F32)
    o_ref[...] = x_ref[...] + _rms(m, g_ref[...])


def _matmul_postnorm(a, w, layer, gains, gain_row, x, *, tm, name):
    M, K = a.shape
    D = w.shape[2]
    return pl.pallas_call(
        _matmul_postnorm_body,
        grid=(M // tm,),
        in_specs=[
            pl.BlockSpec((tm, K), lambda i: (i, 0)),
            pl.BlockSpec((None, K, D), lambda i: (layer, 0, 0)),
            pl.BlockSpec((None, 1, D), lambda i: (gain_row, 0, 0)),
            pl.BlockSpec((tm, D), lambda i: (i, 0)),
        ],
        out_specs=pl.BlockSpec((tm, D), lambda i: (i, 0)),
        out_shape=jax.ShapeDtypeStruct((M, D), F32),
        compiler_params=_params("arbitrary"),
        name=name,
    )(a, w, gains, x)


def _conv_rows(gate, prev8, cw, cb):
    row = lax.broadcasted_iota(jnp.int32, gate.shape, 0)
    p_old, p_new = prev8[6:7], prev8[7:8]
    s1 = jnp.where(row == 0, p_new, pltpu.roll(gate, 1, axis=0))
    s2 = jnp.where(row == 0, p_old, jnp.where(row == 1, p_new, pltpu.roll(gate, 2, axis=0)))
    return cb + cw[0:1] * s2 + cw[1:2] * s1 + cw[2:3] * gate


def _conv_ffn_body(x_ref, g2_ref, wg_ref, wv_ref, cw_ref, cb_ref, wd_ref, st_ref, g3_ref,
                   o_ref, tail_ref, xn_ref, h_ref, carry_ref, *, tm, tf, tn, nf, nn, seg, tiles_per_seq):
    i = pl.program_id(0)
    s = pl.program_id(1)

    @pl.when(s == 0)
    def _():
        xn_ref[...] = _rms(x_ref[...], g2_ref[...]).astype(BF16)

    @pl.when(s < nf)
    def _():
        xn = xn_ref[...]
        half = tf // 2
        for hh in range(2):
            cols = slice(hh * half, (hh + 1) * half)
            gate = jnp.dot(xn, wg_ref[:, cols], preferred_element_type=F32)
            val = jnp.dot(xn, wv_ref[:, cols], preferred_element_type=F32)
            cw = cw_ref[:, cols]
            cb = cb_ref[:, cols]
            if seg >= tm:
                @pl.when(i % tiles_per_seq == 0)
                def _():
                    carry_ref[s, :, cols] = st_ref[0, :, cols]

                conv = _conv_rows(gate, carry_ref[s, :, cols], cw, cb)
                carry_ref[s, :, cols] = gate[tm - 8:]
                tail_ref[0, :, cols] = gate[tm - 8:]
            else:
                parts = []
                for q in range(tm // seg):
                    g_q = gate[q * seg:(q + 1) * seg]
                    parts.append(_conv_rows(g_q, st_ref[q, :, cols], cw, cb))
                    tail_ref[q, :, cols] = g_q[seg - 8:]
                conv = jnp.concatenate(parts, axis=0)
            h = (conv * jax.nn.sigmoid(conv) * val).astype(BF16)
            h_ref[:, pl.ds(pl.multiple_of(s * tf + hh * half, half), half)] = h

    for n in range(nn):
        @pl.when(s == nf + n)
        def _():
            o_ref[:, n * tn:(n + 1) * tn] = jnp.dot(h_ref[...], wd_ref[...], preferred_element_type=F32)

    @pl.when(s == nf + nn - 1)
    def _():
        o_ref[...] = x_ref[...] + _rms(o_ref[...], g3_ref[...])


def _conv_ffn(x, gains, layer, w_up, conv_w, conv_b, w_down, state8, *, seq_len, tm, tf, tn, name):
    M, D = x.shape
    F = w_down.shape[1]
    nf = F // tf
    nn = D // tn
    n_seq = M // seq_len
    up = lambda s: jnp.minimum(s, nf - 1)
    if seq_len >= tm:
        tiles_per_seq = seq_len // tm
        seq_blk = 1
        seq_of = lambda i: i // tiles_per_seq
        n_tail = M // tm
    else:
        tiles_per_seq = 1
        seq_blk = tm // seq_len
        seq_of = lambda i: i
        n_tail = n_seq
    body = functools.partial(_conv_ffn_body, tm=tm, tf=tf, tn=tn, nf=nf, nn=nn, seg=seq_len,
                             tiles_per_seq=tiles_per_seq)
    y, tails = pl.pallas_call(
        body,
        grid=(M // tm, nf + nn),
        in_specs=[
            pl.BlockSpec((tm, D), lambda i, s: (i, 0)),
            pl.BlockSpec((None, 1, D), lambda i, s: (4 * layer + 2, 0, 0)),
            pl.BlockSpec((None, D, tf), lambda i, s: (layer, 0, up(s))),
            pl.BlockSpec((None, D, tf), lambda i, s: (layer, 0, nf + up(s))),
            pl.BlockSpec((None, 3, tf), lambda i, s: (layer, 0, up(s))),
            pl.BlockSpec((None, 1, tf), lambda i, s: (layer, 0, up(s))),
            pl.BlockSpec((None, F, tn), lambda i, s: (layer, 0, jnp.maximum(s - nf, 0))),
            pl.BlockSpec((None, seq_blk, 8, tf), lambda i, s: (layer, seq_of(i), 0, up(s))),
            pl.BlockSpec((None, 1, D), lambda i, s: (4 * layer + 3, 0, 0)),
        ],
        out_specs=[
            pl.BlockSpec((tm, D), lambda i, s: (i, 0)),
            pl.BlockSpec((seq_blk, 8, tf), lambda i, s: (i, 0, up(s))),
        ],
        out_shape=[
            jax.ShapeDtypeStruct((M, D), F32),
            jax.ShapeDtypeStruct((n_tail, 8, F), F32),
        ],
        scratch_shapes=[
            pltpu.VMEM((tm, D), BF16),
            pltpu.VMEM((tm, F), BF16),
            pltpu.VMEM((nf, 8, tf), F32),
        ],
        compiler_params=_params("arbitrary", "arbitrary"),
        name=name,
    )(x, gains, w_up, w_up, conv_w, conv_b, w_down, state8, gains)
    return y, tails[tiles_per_seq - 1::tiles_per_seq]


def _gla_level_halves(unit):
    halves = []
    h = CHUNK // 2
    while h >= 1:
        halves.append(h)
        h //= 2
    return halves


def _gla_constants(unit):
    t = np.arange(unit)[:, None]
    u = np.arange(unit)[None, :]
    same_chunk = (t // CHUNK) == (u // CHUNK)
    blocks = [same_chunk & (u <= t), same_chunk & (u > t)]
    masks = [t == u]
    for h in _gla_level_halves(unit):
        anchor = (t // (2 * h)) * (2 * h) + h - 1
        upper = (t % (2 * h)) >= h
        blocks.append(np.where(upper, (u > anchor) & (u <= t), (u > t) & (u <= anchor)))
        same_group = (t // (2 * h)) == (u // (2 * h))
        masks.append(same_group & upper & ((u % (2 * h)) < h))
    sums = np.concatenate(blocks, axis=0).astype(np.float32)
    pair_mask = np.stack(masks).astype(np.float32)
    return jnp.asarray(sums, BF16), jnp.asarray(pair_mask, F32)


def _gla_scan_body(q_ref, k_ref, v_ref, r_ref, gl_ref, wg_ref, bg_ref, hn_ref, s0_ref, sums_ref, mask_ref,
                   o_ref, sn_ref, S_ref, qin_ref, kdec_ref, oin_ref, dec_ref, *, tb, unit, dk):
    t = pl.program_id(2)

    @pl.when(t == 0)
    def _():
        S_ref[...] = s0_ref[...]

    n_levels = mask_ref.shape[0] - 1
    wg_hi, wg_lo = _split_bf16(wg_ref[...], 2)
    bg = bg_ref[...]
    sums3 = jnp.concatenate([sums_ref[...]] * 3, axis=1)
    scale = dk ** -0.5

    for un in range(tb // unit):
        rows = slice(un * unit, (un + 1) * unit)
        gl = gl_ref[rows, :]
        logits = (jnp.dot(gl, wg_hi, preferred_element_type=F32)
                  + jnp.dot(gl, wg_lo, preferred_element_type=F32) + bg)
        gk = (jnp.minimum(logits, 0.0) - jnp.log1p(jnp.exp(-jnp.abs(logits)))) * (1.0 / GLA_TAU)
        gk3 = jnp.concatenate(_split_bf16(gk, 3), axis=0)
        ex = jnp.exp(jnp.dot(sums3, gk3, preferred_element_type=F32))
        qs = q_ref[rows, :].astype(F32) * scale
        k = k_ref[rows, :].astype(F32)
        qin_ref[rows, :] = (qs * ex[0:unit]).astype(BF16)
        kdec_ref[rows, :] = k * ex[unit:2 * unit]
        for c in range(unit // CHUNK):
            last = c * CHUNK + CHUNK
            dec_ref[un * (unit // CHUNK) + c] = ex[last - SUBLANES:last]
        contract_last = (((1,), (1,)), ((), ()))
        A = lax.dot_general(qs.astype(BF16), k.astype(BF16), contract_last,
                            preferred_element_type=F32) * mask_ref[0]
        for l in range(n_levels):
            e = ex[(2 + l) * unit:(3 + l) * unit]
            a_l = lax.dot_general((qs * e).astype(BF16), (k * e).astype(BF16), contract_last,
                                  preferred_element_type=F32)
            A = A + a_l * mask_ref[1 + l]
        oin_ref[rows, :] = jnp.dot(A.astype(BF16), v_ref[rows, :], preferred_element_type=F32)

    hn = hn_ref[...]
    for c in range(tb // CHUNK):
        rows = slice(c * CHUNK, (c + 1) * CHUNK)
        S = S_ref[...]
        o = oin_ref[rows, :] + jnp.dot(qin_ref[rows, :], S.astype(BF16), preferred_element_type=F32)
        k_dec_t = jnp.transpose(kdec_ref[rows, :]).astype(BF16)
        decay_col = jnp.transpose(dec_ref[c])[:, SUBLANES - 1:SUBLANES]
        S_ref[...] = decay_col * S + jnp.dot(k_dec_t, v_ref[rows, :], preferred_element_type=F32)
        r = r_ref[rows, :].astype(F32)
        o_ref[rows, :] = (_rms(o, hn) * (r * jax.nn.sigmoid(r))).astype(o_ref.dtype)

    @pl.when(t == pl.num_programs(2) - 1)
    def _():
        sn_ref[...] = S_ref[...]


def _gla_scan(proj, w_gate, b_gate, head_norm, s0, layer, *, seq_len, tb, name):
    _, B, H, DK, DV = s0.shape
    M = proj.shape[0]
    nt = seq_len // tb
    unit = min(2 * CHUNK, tb)
    v_blk0 = (2 * H * DK) // DV
    gl_blk = (2 * H * DK + 2 * H * DV) // LANES
    sums, pair_mask = _gla_constants(unit)
    body = functools.partial(_gla_scan_body, tb=tb, unit=unit, dk=DK)
    return pl.pallas_call(
        body,
        grid=(B, H, nt),
        in_specs=[
            pl.BlockSpec((tb, DK), lambda b, h, t: (b * nt + t, h)),
            pl.BlockSpec((tb, DK), lambda b, h, t: (b * nt + t, H + h)),
            pl.BlockSpec((tb, DV), lambda b, h, t: (b * nt + t, v_blk0 + h)),
            pl.BlockSpec((tb, DV), lambda b, h, t: (b * nt + t, v_blk0 + H + h)),
            pl.BlockSpec((tb, LANES), lambda b, h, t: (b * nt + t, gl_blk)),
            pl.BlockSpec((None, LANES, DK), lambda b, h, t: (layer, 0, h)),
            pl.BlockSpec((None, 1, DK), lambda b, h, t: (layer, 0, h)),
            pl.BlockSpec((None, 1, DV), lambda b, h, t: (layer, 0, 0)),
            pl.BlockSpec((None, None, None, DK, DV), lambda b, h, t: (layer, b, h, 0, 0)),
            pl.BlockSpec(sums.shape, lambda b, h, t: (0, 0)),
            pl.BlockSpec(pair_mask.shape, lambda b, h, t: (0, 0, 0)),
        ],
        out_specs=[
            pl.BlockSpec((tb, DV), lambda b, h, t: (b * nt + t, h)),
            pl.BlockSpec((None, None, DK, DV), lambda b, h, t: (b, h, 0, 0)),
        ],
        out_shape=[
            jax.ShapeDtypeStruct((M, H * DV), BF16),
            jax.ShapeDtypeStruct((B, H, DK, DV), F32),
        ],
        scratch_shapes=[
            pltpu.VMEM((DK, DV), F32),
            pltpu.VMEM((tb, DK), BF16),
            pltpu.VMEM((tb, DK), F32),
            pltpu.VMEM((tb, DV), F32),
            pltpu.VMEM((tb // CHUNK, SUBLANES, DK), F32),
        ],
        compiler_params=_params("arbitrary", "arbitrary", "arbitrary"),
        name=name,
    )(proj, proj, proj, proj, proj, w_gate, b_gate, head_norm, s0, sums, pair_mask)


def _band_attention_body(q_ref, k_ref, v_ref, bias_ref, o_ref, *, n_chunks, group, dh, chunk0):
    t = pl.program_id(2)
    rows_q = group * CHUNK
    col = lax.broadcasted_iota(jnp.int32, (rows_q, BAND_PAD), 1)
    bias = bias_ref[...]
    scale = dh ** -0.5

    def chunk(c, carry):
        r0 = pl.multiple_of(c * CHUNK, CHUNK)
        cg = chunk0 + t * n_chunks + c
        start = pl.multiple_of((t * n_chunks + c) * CHUNK, CHUNK)
        qc = q_ref[pl.ds(r0, CHUNK), :]
        q4 = jnp.concatenate([qc[:, g * dh:(g + 1) * dh] for g in range(group)], axis=0)
        kb = k_ref[pl.ds(start, BAND_PAD), :]
        vb = v_ref[pl.ds(start, BAND_PAD), :]
        s = lax.dot_general(q4, kb, (((1,), (1,)), ((), ())), preferred_element_type=F32)
        s = s * scale + bias
        first_valid = WINDOW - cg * CHUNK
        s = jnp.where(col < BAND, jnp.where(col >= first_valid, s, -jnp.inf), -jnp.inf)
        m = jnp.max(s, axis=-1, keepdims=True)
        p = jnp.exp(s - m)
        l = jnp.sum(p, axis=-1, keepdims=True)
        o4 = jnp.dot(p.astype(BF16), vb, preferred_element_type=F32) / l
        for g in range(group):
            o_ref[pl.ds(r0, CHUNK), g * dh:(g + 1) * dh] = o4[g * CHUNK:(g + 1) * CHUNK].astype(o_ref.dtype)
        return carry

    lax.fori_loop(0, n_chunks, chunk, 0)


def _band_attention(q, kv_band, bias, layer, *, seq_len, tq, chunk0, name):
    M = q.shape[0]
    B, tp, _ = kv_band.shape
    hkv = bias.shape[1]
    group = bias.shape[2] // CHUNK
    dh = q.shape[1] // (hkv * group)
    nt = seq_len // tq
    body = functools.partial(_band_attention_body, n_chunks=tq // CHUNK, group=group, dh=dh, chunk0=chunk0)
    return pl.pallas_call(
        body,
        grid=(B, hkv, nt),
        in_specs=[
            pl.BlockSpec((tq, group * dh), lambda b, n, t: (b * nt + t, n)),
            pl.BlockSpec((None, tp, dh), lambda b, n, t: (b, 0, n)),
            pl.BlockSpec((None, tp, dh), lambda b, n, t: (b, 0, hkv + n)),
            pl.BlockSpec((None, None, group * CHUNK, BAND_PAD), lambda b, n, t: (layer, n, 0, 0)),
        ],
        out_specs=pl.BlockSpec((tq, group * dh), lambda b, n, t: (b * nt + t, n)),
        out_shape=jax.ShapeDtypeStruct(q.shape, BF16),
        compiler_params=_params("arbitrary", "arbitrary", "arbitrary"),
        name=name,
    )(q, kv_band, kv_band, bias)


def _bias_table(rel_bias, hkv):
    L, H, n_rel = rel_bias.shape
    max_rel = (n_rel - 1) // 2
    u = np.arange(BAND + CHUNK - 1)
    line = rel_bias[:, :, np.clip(WINDOW + CHUNK - 1 - u, -max_rel, max_rel) + max_rel]
    tab = jnp.stack([line[:, :, CHUNK - 1 - i:CHUNK - 1 - i + BAND] for i in range(CHUNK)], axis=2)
    tab = jnp.pad(tab, ((0, 0), (0, 0), (0, 0), (0, BAND_PAD - BAND)))
    return tab.reshape(L, hkv, (H // hkv) * CHUNK, BAND_PAD).astype(F32)


def _trunk(x, gla_s0, conv_s0, kv_past, wts, *, seq_len):
    M, D = x.shape
    n_seq = M // seq_len
    tm = min(512, M)
    gains = wts["gains"]
    depth = gains.shape[0] // 4
    n_a = gla_s0.shape[0]
    gla_new, conv_new = [], []
    kv = None
    for l in range(depth):
        if l == n_a:
            kv = _norm_matmul(x, wts["kv_norm"], 0, wts["att_w_kv"], 0, tm=tm, tn=1024, out_dtype=F32,
                              name="kv_proj")
            kv3 = kv.reshape(n_seq, seq_len, kv.shape[1]).astype(BF16)
            if kv_past is None:
                band = jnp.pad(kv3, ((0, 0), (WINDOW, BAND_PAD - BAND), (0, 0)))
            else:
                band = jnp.pad(jnp.concatenate([kv_past.astype(BF16), kv3], axis=1),
                               ((0, 0), (0, BAND_PAD - BAND), (0, 0)))
        if l < n_a:
            proj = _norm_matmul(x, gains, 4 * l, wts["gla_w_in"], l, tm=tm, tn=7 * LANES, out_dtype=BF16,
                                name="gla_in_proj")
            o, s_new = _gla_scan(proj, wts["gla_w_gate"], wts["gla_b_gate"], wts["gla_head_norm"], gla_s0, l,
                                 seq_len=seq_len, tb=min(4 * CHUNK, seq_len), name="gla_scan")
            gla_new.append(s_new)
            w_o, w_layer = wts["gla_w_o"], l
        else:
            j = l - n_a
            q = _norm_matmul(x, gains, 4 * l, wts["att_w_q"], j, tm=tm, tn=1024, out_dtype=BF16,
                             name="att_q_proj")
            o = _band_attention(q, band, wts["att_bias"], j, seq_len=seq_len, tq=min(4 * CHUNK, seq_len),
                                chunk0=0 if kv_past is None else LEFT_CHUNKS, name="band_attention")
            w_o, w_layer = wts["att_w_o"], j
        x = _matmul_postnorm(o, w_o, w_layer, gains, 4 * l + 1, x, tm=tm, name="mixer_out_proj")
        x, tail = _conv_ffn(x, gains, l, wts["ffn_w_up"], wts["ffn_conv_w"], wts["ffn_conv_b"],
                            wts["ffn_w_down"], conv_s0, seq_len=seq_len, tm=tm, tf=512, tn=512,
                            name="conv_ffn")
        conv_new.append(tail[:, 6:8])
    return x, jnp.stack(gla_new), jnp.stack(conv_new), kv


def kernel(x_prompt, x_sample, state_gla, state_ffn_conv, cache_k, cache_v, norm_gains, gla_w_in,
           gla_w_gate, gla_b_gate, gla_head_norm, gla_w_o, kv_norm, att_w_kv, att_w_q, att_rel_bias,
           att_w_o, ffn_w_up, ffn_conv_w, ffn_conv_b, ffn_w_down):
    B, S, D = x_prompt.shape
    DB, DS, _ = x_sample.shape
    n_a, _, H, DK, DV = state_gla.shape
    depth = norm_gains.shape[0]
    hkv, dh = cache_k.shape[2], cache_k.shape[3]
    F = ffn_w_down.shape[1]
    assert cache_k.shape[1] == WINDOW and S % CHUNK == 0 and DS == CHUNK
    qkvr = 2 * H * DK + 2 * H * DV
    rank = gla_w_in.shape[2] - qkvr

    wts = dict(
        gains=norm_gains.reshape(depth * 4, 1, D),
        gla_w_in=jnp.pad(gla_w_in, ((0, 0), (0, 0), (0, LANES - rank))).astype(BF16),
        gla_w_gate=jnp.pad(gla_w_gate, ((0, 0), (0, LANES - rank), (0, 0))),
        gla_b_gate=gla_b_gate[:, None, :],
        gla_head_norm=gla_head_norm[:, None, :],
        gla_w_o=gla_w_o.astype(BF16),
        kv_norm=kv_norm[None, None, :],
        att_w_kv=att_w_kv.astype(BF16)[None],
        att_w_q=att_w_q.astype(BF16),
        att_bias=_bias_table(att_rel_bias, hkv),
        att_w_o=att_w_o.astype(BF16),
        ffn_w_up=ffn_w_up.astype(BF16),
        ffn_conv_w=ffn_conv_w,
        ffn_conv_b=ffn_conv_b[:, None, :],
        ffn_w_down=ffn_w_down.astype(BF16),
    )

    y_p, gla_p, conv_p, kv_p = _trunk(
        x_prompt.reshape(B * S, D), jnp.zeros((n_a, B, H, DK, DV), F32),
        jnp.zeros((depth, B, 8, F), F32), None, wts, seq_len=S)
    kv_past = jnp.concatenate([cache_k.reshape(DB, WINDOW, hkv * dh),
                               cache_v.reshape(DB, WINDOW, hkv * dh)], axis=-1)
    state8 = jnp.pad(state_ffn_conv, ((0, 0), (0, 0), (6, 0), (0, 0)))
    y_s, gla_s, conv_s, kv_s = _trunk(
        x_sample.reshape(DB * DS, D), state_gla, state8, kv_past, wts, seq_len=DS)

    keep = min(WINDOW, S)
    kv_p = kv_p.reshape(B, S, 2 * hkv * dh)[:, S - keep:].reshape(B, keep, 2, hkv, dh)
    kv_s = kv_s.reshape(DB, DS, 2, hkv, dh)
    return (y_p.reshape(B, S, D), y_s.reshape(DB, DS, D), gla_p, gla_s, conv_p, conv_s,
            kv_p[:, :, 0], kv_p[:, :, 1], kv_s[:, :, 0], kv_s[:, :, 1])
```

```python
import functools

import numpy as np
import jax
import jax.numpy as jnp
from jax import lax
from jax.experimental import pallas as pl
from jax.experimental.pallas import tpu as pltpu

CHUNK = 64
LEFT_CHUNKS = 8
WINDOW = LEFT_CHUNKS * CHUNK
BAND = WINDOW + CHUNK
LANES = 128
SUBLANES = 8
MXU_DIM = 256
BAND_PAD = ((BAND + LANES - 1) // LANES) * LANES
GLA_TAU = 16.0
EPS = 1e-6
MASKED = -1e30
VMEM_LIMIT = 56 * 1024 * 1024

F32 = jnp.float32
BF16 = jnp.bfloat16


def _params(*sem):
    return pltpu.CompilerParams(dimension_semantics=sem, vmem_limit_bytes=VMEM_LIMIT)


def _rms(x, g):
    ms = jnp.mean(x * x, axis=-1, keepdims=True)
    return x * lax.rsqrt(ms + EPS) * g


def _split_bf16(x, parts):
    out = []
    for _ in range(parts - 1):
        hi = x.astype(BF16)
        out.append(hi)
        x = x - hi.astype(F32)
    out.append(x.astype(BF16))
    return out


def _tile_major(w, tn):
    L, K, N = w.shape
    return w.reshape(L, K, N // tn, tn).transpose(0, 2, 1, 3)


def _norm_matmul_body(x_ref, g_ref, w_ref, o_ref, xn_ref, *, out_scale):
    @pl.when(pl.program_id(1) == 0)
    def _():
        xn_ref[...] = _rms(x_ref[...], g_ref[...]).astype(BF16)

    y = jnp.dot(xn_ref[...], w_ref[...], preferred_element_type=F32)
    if out_scale != 1.0:
        y = y * out_scale
    o_ref[...] = y.astype(o_ref.dtype)


def _norm_matmul(x, gains, gain_row, w, layer, *, tm, out_dtype, name, out_scale=1.0):
    M, D = x.shape
    _, nj, _, tn = w.shape
    return pl.pallas_call(
        functools.partial(_norm_matmul_body, out_scale=out_scale),
        grid=(M // tm, nj),
        in_specs=[
            pl.BlockSpec((tm, D), lambda i, j: (i, 0)),
            pl.BlockSpec((None, 1, D), lambda i, j: (gain_row, 0, 0)),
            pl.BlockSpec((None, None, D, tn), lambda i, j: (layer, j, 0, 0)),
        ],
        out_specs=pl.BlockSpec((tm, tn), lambda i, j: (i, j)),
        out_shape=jax.ShapeDtypeStruct((M, nj * tn), out_dtype),
        scratch_shapes=[pltpu.VMEM((tm, D), BF16)],
        compiler_params=_params("arbitrary", "arbitrary"),
        name=name,
    )(x, gains, w)


def _matmul_postnorm_body(a_ref, w_ref, g_ref, x_ref, o_ref):
    m = jnp.dot(a_ref[...], w_ref[...], preferred_element_type=F32)
    o_ref[...] = x_ref[...] + _rms(m, g_ref[...])


def _matmul_postnorm(a, w, layer, gains, gain_row, x, *, tm, name):
    M, K = a.shape
    D = w.shape[2]
    return pl.pallas_call(
        _matmul_postnorm_body,
        grid=(M // tm,),
        in_specs=[
            pl.BlockSpec((tm, K), lambda i: (i, 0)),
            pl.BlockSpec((None, K, D), lambda i: (layer, 0, 0)),
            pl.BlockSpec((None, 1, D), lambda i: (gain_row, 0, 0)),
            pl.BlockSpec((tm, D), lambda i: (i, 0)),
        ],
        out_specs=pl.BlockSpec((tm, D), lambda i: (i, 0)),
        out_shape=jax.ShapeDtypeStruct((M, D), F32),
        compiler_params=_params("arbitrary"),
        name=name,
    )(a, w, gains, x)


def _conv_rows(gate, prev8, cw, cb):
    row = lax.broadcasted_iota(jnp.int32, gate.shape, 0)
    p_old, p_new = prev8[6:7], prev8[7:8]
    s1 = jnp.where(row == 0, p_new, pltpu.roll(gate, 1, axis=0))
    s2 = jnp.where(row == 0, p_old, jnp.where(row == 1, p_new, pltpu.roll(gate, 2, axis=0)))
    return cb + cw[0:1] * s2 + cw[1:2] * s1 + cw[2:3] * gate


def _conv_ffn_body(x_ref, g2_ref, wg_ref, wv_ref, cw_ref, cb_ref, wd_ref, st_ref, g3_ref,
                   o_ref, tail_ref, xn_ref, h_ref, carry_ref, *, tm, tf, tn, nf, nn, seg, tiles_per_seq):
    i = pl.program_id(0)
    s = pl.program_id(1)

    @pl.when(s == 0)
    def _():
        xn_ref[...] = _rms(x_ref[...], g2_ref[...]).astype(BF16)

    @pl.when(s < nf)
    def _():
        xn = xn_ref[...]
        half = tf // 2
        for hh in range(2):
            cols = slice(hh * half, (hh + 1) * half)
            gate = jnp.dot(xn, wg_ref[:, cols], preferred_element_type=F32)
            val = jnp.dot(xn, wv_ref[:, cols], preferred_element_type=F32)
            cw = cw_ref[:, cols]
            cb = cb_ref[:, cols]
            if seg >= tm:
                @pl.when(i % tiles_per_seq == 0)
                def _():
                    carry_ref[s, :, cols] = st_ref[0, :, cols]

                conv = _conv_rows(gate, carry_ref[s, :, cols], cw, cb)
                carry_ref[s, :, cols] = gate[tm - 8:]
                tail_ref[0, :, cols] = gate[tm - 8:]
            else:
                parts = []
                for q in range(tm // seg):
                    g_q = gate[q * seg:(q + 1) * seg]
                    parts.append(_conv_rows(g_q, st_ref[q, :, cols], cw, cb))
                    tail_ref[q, :, cols] = g_q[seg - 8:]
                conv = jnp.concatenate(parts, axis=0)
            h = (conv * jax.nn.sigmoid(conv) * val).astype(BF16)
            h_ref[:, pl.ds(pl.multiple_of(s * tf + hh * half, half), half)] = h

    for n in range(nn):
        @pl.when(s == nf + n)
        def _():
            o_ref[:, n * tn:(n + 1) * tn] = jnp.dot(h_ref[...], wd_ref[...], preferred_element_type=F32)

    @pl.when(s == nf + nn - 1)
    def _():
        o_ref[...] = x_ref[...] + _rms(o_ref[...], g3_ref[...])


def _conv_ffn(x, gains, layer, w_up, conv_w, conv_b, w_down, state8, *, seq_len, tm, name):
    M, D = x.shape
    _, nf2, _, tf = w_up.shape
    _, nn, F, tn = w_down.shape
    nf = nf2 // 2
    n_seq = M // seq_len
    up = lambda s: jnp.minimum(s, nf - 1)
    if seq_len >= tm:
        tiles_per_seq = seq_len // tm
        seq_blk = 1
        seq_of = lambda i: i // tiles_per_seq
        n_tail = M // tm
    else:
        tiles_per_seq = 1
        seq_blk = tm // seq_len
        seq_of = lambda i: i
        n_tail = n_seq
    body = functools.partial(_conv_ffn_body, tm=tm, tf=tf, tn=tn, nf=nf, nn=nn, seg=seq_len,
                             tiles_per_seq=tiles_per_seq)
    y, tails = pl.pallas_call(
        body,
        grid=(M // tm, nf + nn),
        in_specs=[
            pl.BlockSpec((tm, D), lambda i, s: (i, 0)),
            pl.BlockSpec((None, 1, D), lambda i, s: (4 * layer + 2, 0, 0)),
            pl.BlockSpec((None, None, D, tf), lambda i, s: (layer, up(s), 0, 0)),
            pl.BlockSpec((None, None, D, tf), lambda i, s: (layer, nf + up(s), 0, 0)),
            pl.BlockSpec((None, 3, tf), lambda i, s: (layer, 0, up(s))),
            pl.BlockSpec((None, 1, tf), lambda i, s: (layer, 0, up(s))),
            pl.BlockSpec((None, None, F, tn), lambda i, s: (layer, jnp.maximum(s - nf, 0), 0, 0)),
            pl.BlockSpec((None, seq_blk, 8, tf), lambda i, s: (layer, seq_of(i), 0, up(s))),
            pl.BlockSpec((None, 1, D), lambda i, s: (4 * layer + 3, 0, 0)),
        ],
        out_specs=[
            pl.BlockSpec((tm, D), lambda i, s: (i, 0)),
            pl.BlockSpec((seq_blk, 8, tf), lambda i, s: (i, 0, up(s))),
        ],
        out_shape=[
            jax.ShapeDtypeStruct((M, D), F32),
            jax.ShapeDtypeStruct((n_tail, 8, F), F32),
        ],
        scratch_shapes=[
            pltpu.VMEM((tm, D), BF16),
            pltpu.VMEM((tm, F), BF16),
            pltpu.VMEM((nf, 8, tf), F32),
        ],
        compiler_params=_params("arbitrary", "arbitrary"),
        name=name,
    )(x, gains, w_up, w_up, conv_w, conv_b, w_down, state8, gains)
    return y, tails[tiles_per_seq - 1::tiles_per_seq]


def _gla_levels(unit):
    halves = []
    h = unit // 2
    while h >= 1:
        halves.append(h)
        h //= 2
    return [h for h in halves if h >= SUBLANES], [h for h in halves if h < SUBLANES]


def _gla_constants(unit):
    big, small = _gla_levels(unit)
    t = np.arange(unit)[:, None]
    u = np.arange(unit)[None, :]
    blocks = [u <= t]
    masks = [t == u]
    for h in big + small:
        anchor = (t // (2 * h)) * (2 * h) + h - 1
        upper = (t % (2 * h)) >= h
        if h in small:
            blocks.append(np.where(upper, (u > anchor) & (u <= t), (u > t) & (u <= anchor)))
        same_group = (t // (2 * h)) == (u // (2 * h))
        masks.append(same_group & upper & ((u % (2 * h)) < h))
    sums = np.concatenate(blocks, axis=0).astype(np.float32)
    pair_mask = np.stack(masks).astype(np.float32)
    return jnp.asarray(sums, BF16), jnp.asarray(pair_mask, F32)


def _gla_scan_body(q_ref, k_ref, v_ref, r_ref, gl_ref, wg_ref, bg_ref, hn_ref, s0_ref, sums_ref, mask_ref,
                   o_ref, sn_ref, S_ref, *, tb, unit, dk):
    t = pl.program_id(2)

    @pl.when(t == 0)
    def _():
        S_ref[...] = s0_ref[...]

    big, small = _gla_levels(unit)
    wg_hi, wg_lo = _split_bf16(wg_ref[...], 2)
    bg = bg_ref[...]
    hn = hn_ref[...]
    sums2 = jnp.concatenate([sums_ref[...]] * 2, axis=1)
    scale = dk ** -0.5
    row = lax.broadcasted_iota(jnp.int32, (unit, 1), 0)
    contract_last = (((1,), (1,)), ((), ()))

    for un in range(tb // unit):
        rows = slice(un * unit, (un + 1) * unit)
        gl = gl_ref[rows, :]
        logits = (jnp.dot(gl, wg_hi, preferred_element_type=F32)
                  + jnp.dot(gl, wg_lo, preferred_element_type=F32) + bg)
        gk = (jnp.minimum(logits, 0.0) - jnp.log1p(jnp.exp(-jnp.abs(logits)))) * (1.0 / GLA_TAU)
        gk2 = jnp.concatenate(_split_bf16(gk, 2), axis=0)
        sm = jnp.dot(sums2, gk2, preferred_element_type=F32)
        b = sm[0:unit]
        b_end = b[unit - 1:unit]
        qs = q_ref[rows, :].astype(F32) * scale
        k = k_ref[rows, :].astype(F32)
        v = v_ref[rows, :]

        A = lax.dot_general(qs.astype(BF16), k.astype(BF16), contract_last,
                            preferred_element_type=F32) * mask_ref[0]
        for li, h in enumerate(big + small):
            if h in big:
                anchors = [b[g * 2 * h + h - 1:g * 2 * h + h] for g in range(unit // (2 * h))]
                anc = jnp.concatenate([jnp.broadcast_to(a, (2 * h, dk)) for a in anchors], axis=0)
                d = b - anc
                e = jnp.exp(jnp.where((row % (2 * h)) >= h, d, -d))
            else:
                m = 1 + small.index(h)
                e = jnp.exp(sm[m * unit:(m + 1) * unit])
            a_l = lax.dot_general((qs * e).astype(BF16), (k * e).astype(BF16), contract_last,
                                  preferred_element_type=F32)
            A = A + a_l * mask_ref[1 + li]

        S = S_ref[...]
        o = (jnp.dot(A.astype(BF16), v, preferred_element_type=F32)
             + jnp.dot((qs * jnp.exp(b)).astype(BF16), S.astype(BF16), preferred_element_type=F32))
        k_dec_t = jnp.transpose(k * jnp.exp(b_end - b)).astype(BF16)
        decay_col = jnp.transpose(jnp.exp(b[unit - SUBLANES:]))[:, SUBLANES - 1:]
        S_ref[...] = decay_col * S + jnp.dot(k_dec_t, v, preferred_element_type=F32)
        r = r_ref[rows, :].astype(F32)
        o_ref[rows, :] = (_rms(o, hn) * (r * jax.nn.sigmoid(r))).astype(o_ref.dtype)

    @pl.when(t == pl.num_programs(2) - 1)
    def _():
        sn_ref[...] = S_ref[...]


def _gla_scan(proj, w_gate, b_gate, head_norm, s0, layer, *, seq_len, tb, name):
    _, B, H, DK, DV = s0.shape
    M = proj.shape[0]
    nt = seq_len // tb
    unit = min(LANES, tb)
    v_blk0 = (2 * H * DK) // DV
    gl_blk = (2 * H * DK + 2 * H * DV) // LANES
    sums, pair_mask = _gla_constants(unit)
    body = functools.partial(_gla_scan_body, tb=tb, unit=unit, dk=DK)
    return pl.pallas_call(
        body,
        grid=(B, H, nt),
        in_specs=[
            pl.BlockSpec((tb, DK), lambda b, h, t: (b * nt + t, h)),
            pl.BlockSpec((tb, DK), lambda b, h, t: (b * nt + t, H + h)),
            pl.BlockSpec((tb, DV), lambda b, h, t: (b * nt + t, v_blk0 + h)),
            pl.BlockSpec((tb, DV), lambda b, h, t: (b * nt + t, v_blk0 + H + h)),
            pl.BlockSpec((tb, LANES), lambda b, h, t: (b * nt + t, gl_blk)),
            pl.BlockSpec((None, LANES, DK), lambda b, h, t: (layer, 0, h)),
            pl.BlockSpec((None, 1, DK), lambda b, h, t: (layer, 0, h)),
            pl.BlockSpec((None, 1, DV), lambda b, h, t: (layer, 0, 0)),
            pl.BlockSpec((None, None, None, DK, DV), lambda b, h, t: (layer, b, h, 0, 0)),
            pl.BlockSpec(sums.shape, lambda b, h, t: (0, 0)),
            pl.BlockSpec(pair_mask.shape, lambda b, h, t: (0, 0, 0)),
        ],
        out_specs=[
            pl.BlockSpec((tb, DV), lambda b, h, t: (b * nt + t, h)),
            pl.BlockSpec((None, None, DK, DV), lambda b, h, t: (b, h, 0, 0)),
        ],
        out_shape=[
            jax.ShapeDtypeStruct((M, H * DV), BF16),
            jax.ShapeDtypeStruct((B, H, DK, DV), F32),
        ],
        scratch_shapes=[pltpu.VMEM((DK, DV), F32)],
        compiler_params=_params("arbitrary", "arbitrary", "arbitrary"),
        name=name,
    )(proj, proj, proj, proj, proj, w_gate, b_gate, head_norm, s0, sums, pair_mask)


def _band_attention_body(q_ref, k_ref, v_ref, bias_ref, o_ref, *, n_chunks, group, dh, chunk0):
    t = pl.program_id(2)
    rows_q = group * CHUNK
    bias = bias_ref[...]
    ones = jnp.ones((BAND_PAD, dh), BF16)

    def chunks(clip_start):
        for c in range(n_chunks):
            r0 = c * CHUNK
            start = pl.multiple_of((t * n_chunks + c) * CHUNK, CHUNK)
            qc = q_ref[r0:r0 + CHUNK, :]
            q4 = jnp.concatenate([qc[:, g * dh:(g + 1) * dh] for g in range(group)], axis=0)
            kb = k_ref[pl.ds(start, BAND_PAD), :]
            vb = jnp.concatenate([v_ref[pl.ds(start, BAND_PAD), :], ones], axis=1)
            s = lax.dot_general(q4, kb, (((1,), (1,)), ((), ())), preferred_element_type=F32) + bias
            if clip_start:
                first_valid = WINDOW - (chunk0 + t * n_chunks + c) * CHUNK
                col = lax.broadcasted_iota(jnp.int32, (rows_q, BAND_PAD), 1)
                s = jnp.where(col >= first_valid, s, -jnp.inf)
            m = jnp.max(s, axis=-1, keepdims=True)
            p = jnp.exp(s - m).astype(BF16)
            ov = jnp.dot(p, vb, preferred_element_type=F32)
            o4 = ov[:, :dh] / ov[:, dh:]
            for g in range(group):
                o_ref[r0:r0 + CHUNK, g * dh:(g + 1) * dh] = o4[g * CHUNK:(g + 1) * CHUNK].astype(o_ref.dtype)

    starts_early = (chunk0 + t * n_chunks) < LEFT_CHUNKS

    @pl.when(starts_early)
    def _():
        chunks(True)

    @pl.when(jnp.logical_not(starts_early))
    def _():
        chunks(False)


def _band_attention(q, kv_band, bias, layer, *, seq_len, tq, chunk0, name):
    M = q.shape[0]
    B, tp, _ = kv_band.shape
    hkv = bias.shape[1]
    group = bias.shape[2] // CHUNK
    dh = q.shape[1] // (hkv * group)
    nt = seq_len // tq
    body = functools.partial(_band_attention_body, n_chunks=tq // CHUNK, group=group, dh=dh, chunk0=chunk0)
    return pl.pallas_call(
        body,
        grid=(B, hkv, nt),
        in_specs=[
            pl.BlockSpec((tq, group * dh), lambda b, n, t: (b * nt + t, n)),
            pl.BlockSpec((None, tp, dh), lambda b, n, t: (b, 0, n)),
            pl.BlockSpec((None, tp, dh), lambda b, n, t: (b, 0, hkv + n)),
            pl.BlockSpec((None, None, group * CHUNK, BAND_PAD), lambda b, n, t: (layer, n, 0, 0)),
        ],
        out_specs=pl.BlockSpec((tq, group * dh), lambda b, n, t: (b * nt + t, n)),
        out_shape=jax.ShapeDtypeStruct(q.shape, BF16),
        compiler_params=_params("arbitrary", "arbitrary", "arbitrary"),
        name=name,
    )(q, kv_band, kv_band, bias)


def _bias_table(rel_bias, hkv):
    L, H, n_rel = rel_bias.shape
    max_rel = (n_rel - 1) // 2
    u = np.arange(BAND + CHUNK - 1)
    line = rel_bias[:, :, np.clip(WINDOW + CHUNK - 1 - u, -max_rel, max_rel) + max_rel]
    tab = jnp.stack([line[:, :, CHUNK - 1 - i:CHUNK - 1 - i + BAND] for i in range(CHUNK)], axis=2)
    tab = jnp.pad(tab.astype(F32), ((0, 0), (0, 0), (0, 0), (0, BAND_PAD - BAND)), constant_values=MASKED)
    return tab.reshape(L, hkv, (H // hkv) * CHUNK, BAND_PAD)


def _trunk(x, gla_s0, conv_s0, kv_past, wts, *, seq_len):
    M, D = x.shape
    n_seq = M // seq_len
    tm = min(512, M)
    tm_proj = min(1024, M)
    gains = wts["gains"]
    depth = gains.shape[0] // 4
    n_a = gla_s0.shape[0]
    gla_new, conv_new = [], []
    kv = None
    for l in range(depth):
        if l == n_a:
            kv = _norm_matmul(x, wts["kv_norm"], 0, wts["att_w_kv"], 0, tm=tm_proj, out_dtype=F32,
                              name="kv_proj")
            kv3 = kv.reshape(n_seq, seq_len, kv.shape[1]).astype(BF16)
            if kv_past is None:
                band = jnp.pad(kv3, ((0, 0), (WINDOW, BAND_PAD - BAND), (0, 0)))
            else:
                band = jnp.pad(jnp.concatenate([kv_past.astype(BF16), kv3], axis=1),
                               ((0, 0), (0, BAND_PAD - BAND), (0, 0)))
        if l < n_a:
            proj = _norm_matmul(x, gains, 4 * l, wts["gla_w_in"], l, tm=tm_proj, out_dtype=BF16,
                                name="gla_in_proj")
            o, s_new = _gla_scan(proj, wts["gla_w_gate"], wts["gla_b_gate"], wts["gla_head_norm"], gla_s0, l,
                                 seq_len=seq_len, tb=min(4 * CHUNK, seq_len), name="gla_scan")
            gla_new.append(s_new)
            w_o, w_layer = wts["gla_w_o"], l
        else:
            j = l - n_a
            q = _norm_matmul(x, gains, 4 * l, wts["att_w_q"], j, tm=tm_proj, out_dtype=BF16, name="att_q_proj",
                             out_scale=wts["att_scale"])
            o = _band_attention(q, band, wts["att_bias"], j, seq_len=seq_len, tq=min(4 * CHUNK, seq_len),
                                chunk0=0 if kv_past is None else LEFT_CHUNKS, name="band_attention")
            w_o, w_layer = wts["att_w_o"], j
        x = _matmul_postnorm(o, w_o, w_layer, gains, 4 * l + 1, x, tm=tm, name="mixer_out_proj")
        x, tail = _conv_ffn(x, gains, l, wts["ffn_w_up"], wts["ffn_conv_w"], wts["ffn_conv_b"],
                            wts["ffn_w_down"], conv_s0, seq_len=seq_len, tm=tm, name="conv_ffn")
        conv_new.append(tail[:, 6:8])
    return x, jnp.stack(gla_new), jnp.stack(conv_new), kv


def kernel(x_prompt, x_sample, state_gla, state_ffn_conv, cache_k, cache_v, norm_gains, gla_w_in,
           gla_w_gate, gla_b_gate, gla_head_norm, gla_w_o, kv_norm, att_w_kv, att_w_q, att_rel_bias,
           att_w_o, ffn_w_up, ffn_conv_w, ffn_conv_b, ffn_w_down):
    B, S, D = x_prompt.shape
    DB, DS, _ = x_sample.shape
    n_a, _, H, DK, DV = state_gla.shape
    depth = norm_gains.shape[0]
    hkv, dh = cache_k.shape[2], cache_k.shape[3]
    F = ffn_w_down.shape[1]
    assert cache_k.shape[1] == WINDOW and S % CHUNK == 0 and DS == CHUNK
    qkvr = 2 * H * DK + 2 * H * DV
    rank = gla_w_in.shape[2] - qkvr
    tn_in = 5 * MXU_DIM
    in_pad = -(qkvr + LANES) % tn_in + LANES - rank

    wts = dict(
        gains=norm_gains.reshape(depth * 4, 1, D),
        gla_w_in=_tile_major(jnp.pad(gla_w_in, ((0, 0), (0, 0), (0, in_pad))).astype(BF16), tn_in),
        gla_w_gate=jnp.pad(gla_w_gate, ((0, 0), (0, LANES - rank), (0, 0))),
        gla_b_gate=gla_b_gate[:, None, :],
        gla_head_norm=gla_head_norm[:, None, :],
        gla_w_o=gla_w_o.astype(BF16),
        kv_norm=kv_norm[None, None, :],
        att_w_kv=_tile_major(att_w_kv.astype(BF16)[None], 4 * MXU_DIM),
        att_w_q=_tile_major(att_w_q.astype(BF16), 4 * MXU_DIM),
        att_bias=_bias_table(att_rel_bias, hkv),
        att_scale=dh ** -0.5,
        att_w_o=att_w_o.astype(BF16),
        ffn_w_up=_tile_major(ffn_w_up.astype(BF16), 2 * MXU_DIM),
        ffn_conv_w=ffn_conv_w,
        ffn_conv_b=ffn_conv_b[:, None, :],
        ffn_w_down=_tile_major(ffn_w_down.astype(BF16), 2 * MXU_DIM),
    )

    y_p, gla_p, conv_p, kv_p = _trunk(
        x_prompt.reshape(B * S, D), jnp.zeros((n_a, B, H, DK, DV), F32),
        jnp.zeros((depth, B, 8, F), F32), None, wts, seq_len=S)
    kv_past = jnp.concatenate([cache_k.reshape(DB, WINDOW, hkv * dh),
                               cache_v.reshape(DB, WINDOW, hkv * dh)], axis=-1)
    state8 = jnp.pad(state_ffn_conv, ((0, 0), (0, 0), (6, 0), (0, 0)))
    y_s, gla_s, conv_s, kv_s = _trunk(
        x_sample.reshape(DB * DS, D), state_gla, state8, kv_past, wts, seq_len=DS)

    keep = min(WINDOW, S)
    kv_p = kv_p.reshape(B, S, 2 * hkv * dh)[:, S - keep:].reshape(B, keep, 2, hkv, dh)
    kv_s = kv_s.reshape(DB, DS, 2, hkv, dh)
    return (y_p.reshape(B, S, D), y_s.reshape(DB, DS, D), gla_p, gla_s, conv_p, conv_s,
            kv_p[:, :, 0], kv_p[:, :, 1], kv_s[:, :, 0], kv_s[:, :, 1])
```

```python
import functools

import numpy as np
import jax
import jax.numpy as jnp
from jax import lax
from jax.experimental import pallas as pl
from jax.experimental.pallas import tpu as pltpu

CHUNK = 64
LEFT_CHUNKS = 8
WINDOW = LEFT_CHUNKS * CHUNK
BAND = WINDOW + CHUNK
LANES = 128
SUBLANES = 8
MXU_DIM = 256
BAND_PAD = ((BAND + LANES - 1) // LANES) * LANES
GLA_TAU = 16.0
EPS = 1e-6
MASKED = -1e30
VMEM_LIMIT = 56 * 1024 * 1024

F32 = jnp.float32
BF16 = jnp.bfloat16


def _params(*sem):
    return pltpu.CompilerParams(dimension_semantics=sem, vmem_limit_bytes=VMEM_LIMIT)


def _rms(x, g):
    ms = jnp.mean(x * x, axis=-1, keepdims=True)
    return x * lax.rsqrt(ms + EPS) * g


def _split_bf16(x, parts):
    out = []
    for _ in range(parts - 1):
        hi = x.astype(BF16)
        out.append(hi)
        x = x - hi.astype(F32)
    out.append(x.astype(BF16))
    return out


def _tile_major(w, tn):
    L, K, N = w.shape
    return w.reshape(L, K, N // tn, tn).transpose(0, 2, 1, 3)


def _norm_matmul_body(x_ref, g_ref, w_ref, o_ref, xn_ref, *, out_scale):
    @pl.when(pl.program_id(1) == 0)
    def _():
        xn_ref[...] = _rms(x_ref[...], g_ref[...]).astype(BF16)

    y = jnp.dot(xn_ref[...], w_ref[...], preferred_element_type=F32)
    if out_scale != 1.0:
        y = y * out_scale
    o_ref[...] = y.astype(o_ref.dtype)


def _norm_matmul(x, gains, gain_row, w, layer, *, tm, out_dtype, name, out_scale=1.0):
    M, D = x.shape
    _, nj, _, tn = w.shape
    return pl.pallas_call(
        functools.partial(_norm_matmul_body, out_scale=out_scale),
        grid=(M // tm, nj),
        in_specs=[
            pl.BlockSpec((tm, D), lambda i, j: (i, 0)),
            pl.BlockSpec((None, 1, D), lambda i, j: (gain_row, 0, 0)),
            pl.BlockSpec((None, None, D, tn), lambda i, j: (layer, j, 0, 0)),
        ],
        out_specs=pl.BlockSpec((tm, tn), lambda i, j: (i, j)),
        out_shape=jax.ShapeDtypeStruct((M, nj * tn), out_dtype),
        scratch_shapes=[pltpu.VMEM((tm, D), BF16)],
        compiler_params=_params("arbitrary", "arbitrary"),
        name=name,
    )(x, gains, w)


def _matmul_postnorm_body(a_ref, w_ref, g_ref, x_ref, o_ref):
    m = jnp.dot(a_ref[...], w_ref[...], preferred_element_type=F32)
    o_ref[...] = x_ref[...] + _rms(m, g_ref[...])


def _matmul_postnorm(a, w, layer, gains, gain_row, x, *, tm, name):
    M, K = a.shape
    D = w.shape[2]
    return pl.pallas_call(
        _matmul_postnorm_body,
        grid=(M // tm,),
        in_specs=[
            pl.BlockSpec((tm, K), lambda i: (i, 0)),
            pl.BlockSpec((None, K, D), lambda i: (layer, 0, 0)),
            pl.BlockSpec((None, 1, D), lambda i: (gain_row, 0, 0)),
            pl.BlockSpec((tm, D), lambda i: (i, 0)),
        ],
        out_specs=pl.BlockSpec((tm, D), lambda i: (i, 0)),
        out_shape=jax.ShapeDtypeStruct((M, D), F32),
        compiler_params=_params("arbitrary"),
        name=name,
    )(a, w, gains, x)


def _ffn_up_matmul(xn_ref, wg_ref, wv_ref, gv_ref, slot):
    xn = xn_ref[...]
    gv_ref[slot, 0, SUBLANES:] = jnp.dot(xn, wg_ref[...], preferred_element_type=F32)
    gv_ref[slot, 1, SUBLANES:] = jnp.dot(xn, wv_ref[...], preferred_element_type=F32)


def _ffn_gate_tile(f, slot, gv_ref, cwb_ref, st_ref, tail_ref, carry_ref, h_ref, *, tm, tf, seg):
    if isinstance(f, int):
        cols = slice(f * tf, (f + 1) * tf)
    else:
        cols = pl.ds(pl.multiple_of(f * tf, tf), tf)
    cwb = cwb_ref[:, cols]
    cw_old, cw_mid, cw_new, cb = cwb[0:1], cwb[1:2], cwb[2:3], cwb[3:4]
    g_ref = gv_ref.at[slot, 0]
    v_ref = gv_ref.at[slot, 1]
    lead = SUBLANES
    slab = min(seg, CHUNK)
    g_ref[0:lead] = carry_ref[:, cols] if seg >= tm else st_ref[0, :, cols]
    for r in range(tm // slab):
        r0 = lead + r * slab
        g_new = g_ref[r0:r0 + slab]
        g_mid = g_ref[r0 - 1:r0 - 1 + slab]
        g_old = g_ref[r0 - 2:r0 - 2 + slab]
        if seg < tm:
            prev = st_ref[r, :, cols]
            row = lax.broadcasted_iota(jnp.int32, (slab, tf), 0)
            g_mid = jnp.where(row == 0, prev[7:8], g_mid)
            g_old = jnp.where(row == 0, prev[6:7], jnp.where(row == 1, prev[7:8], g_old))
            tail_ref[r, :, cols] = g_ref[r0 + slab - 8:r0 + slab]
        conv = cb + cw_old * g_old + cw_mid * g_mid + cw_new * g_new
        h = conv * jax.nn.sigmoid(conv) * v_ref[r0:r0 + slab]
        h_ref[r * slab:(r + 1) * slab, cols] = h.astype(BF16)
    if seg >= tm:
        last = g_ref[tm:tm + lead]
        carry_ref[:, cols] = last
        tail_ref[0, :, cols] = last


def _conv_ffn_body(x_ref, g2_ref, wg_ref, wv_ref, cwb_ref, wd_ref, st_ref, g3_ref,
                   o_ref, tail_ref, xn_ref, h_ref, gv_ref, carry_ref, *, tm, tf, tn, nf, nn, seg, tiles_per_seq):
    i = pl.program_id(0)
    s = pl.program_id(1)
    gate_tile = functools.partial(_ffn_gate_tile, gv_ref=gv_ref, cwb_ref=cwb_ref, st_ref=st_ref,
                                  tail_ref=tail_ref, carry_ref=carry_ref, h_ref=h_ref, tm=tm, tf=tf, seg=seg)

    @pl.when(s == 0)
    def _():
        xn_ref[...] = _rms(x_ref[...], g2_ref[...]).astype(BF16)
        if seg >= tm:
            @pl.when(i % tiles_per_seq == 0)
            def _():
                carry_ref[...] = st_ref[0]

        _ffn_up_matmul(xn_ref, wg_ref, wv_ref, gv_ref, 0)

    for par in range(2):
        @pl.when((s >= 1) & (s < nf) & (s % 2 == par))
        def _():
            gate_tile(s - 1, 1 - par)
            _ffn_up_matmul(xn_ref, wg_ref, wv_ref, gv_ref, par)

    for n in range(nn):
        @pl.when(s == nf + n)
        def _():
            if n == 0:
                gate_tile(nf - 1, (nf - 1) % 2)
            o_ref[:, n * tn:(n + 1) * tn] = jnp.dot(h_ref[...], wd_ref[...], preferred_element_type=F32)

    @pl.when(s == nf + nn - 1)
    def _():
        o_ref[...] = x_ref[...] + _rms(o_ref[...], g3_ref[...])


def _conv_ffn(x, gains, layer, w_up, conv_wb, w_down, state8, *, seq_len, tm, tf, tn, name):
    M, D = x.shape
    F = w_down.shape[1]
    nf = F // tf
    nn = D // tn
    n_seq = M // seq_len
    up = lambda s: jnp.minimum(s, nf - 1)
    if seq_len >= tm:
        tiles_per_seq = seq_len // tm
        seq_blk = 1
        seq_of = lambda i: i // tiles_per_seq
        n_tail = M // tm
    else:
        tiles_per_seq = 1
        seq_blk = tm // seq_len
        seq_of = lambda i: i
        n_tail = n_seq
    body = functools.partial(_conv_ffn_body, tm=tm, tf=tf, tn=tn, nf=nf, nn=nn, seg=seq_len,
                             tiles_per_seq=tiles_per_seq)
    y, tails = pl.pallas_call(
        body,
        grid=(M // tm, nf + nn),
        in_specs=[
            pl.BlockSpec((tm, D), lambda i, s: (i, 0)),
            pl.BlockSpec((None, 1, D), lambda i, s: (4 * layer + 2, 0, 0)),
            pl.BlockSpec((None, D, tf), lambda i, s: (layer, 0, up(s))),
            pl.BlockSpec((None, D, tf), lambda i, s: (layer, 0, nf + up(s))),
            pl.BlockSpec((None, 4, F), lambda i, s: (layer, 0, 0)),
            pl.BlockSpec((None, F, tn), lambda i, s: (layer, 0, jnp.maximum(s - nf, 0))),
            pl.BlockSpec((None, seq_blk, 8, F), lambda i, s: (layer, seq_of(i), 0, 0)),
            pl.BlockSpec((None, 1, D), lambda i, s: (4 * layer + 3, 0, 0)),
        ],
        out_specs=[
            pl.BlockSpec((tm, D), lambda i, s: (i, 0)),
            pl.BlockSpec((seq_blk, 8, F), lambda i, s: (i, 0, 0)),
        ],
        out_shape=[
            jax.ShapeDtypeStruct((M, D), F32),
            jax.ShapeDtypeStruct((n_tail, 8, F), F32),
        ],
        scratch_shapes=[
            pltpu.VMEM((tm, D), BF16),
            pltpu.VMEM((tm, F), BF16),
            pltpu.VMEM((2, 2, SUBLANES + tm, tf), F32),
            pltpu.VMEM((8, F), F32),
        ],
        compiler_params=_params("arbitrary", "arbitrary"),
        name=name,
    )(x, gains, w_up, w_up, conv_wb, w_down, state8, gains)
    return y, tails[tiles_per_seq - 1::tiles_per_seq]


def _gla_levels(unit):
    halves = []
    h = unit // 2
    while h >= 1:
        halves.append(h)
        h //= 2
    return [h for h in halves if h >= SUBLANES], [h for h in halves if h < SUBLANES]


def _gla_constants(unit):
    big, small = _gla_levels(unit)
    t = np.arange(unit)[:, None]
    u = np.arange(unit)[None, :]
    blocks = [u <= t]
    masks = [t == u]
    for h in big + small:
        anchor = (t // (2 * h)) * (2 * h) + h - 1
        upper = (t % (2 * h)) >= h
        if h in small:
            blocks.append(np.where(upper, (u > anchor) & (u <= t), (u > t) & (u <= anchor)))
        same_group = (t // (2 * h)) == (u // (2 * h))
        masks.append(same_group & upper & ((u % (2 * h)) < h))
    sums = np.concatenate(blocks, axis=0).astype(np.float32)
    pair_mask = np.stack(masks).astype(np.float32)
    return jnp.asarray(sums, BF16), jnp.asarray(pair_mask, F32)


def _gla_scan_body(q_ref, k_ref, v_ref, r_ref, gl_ref, wg_ref, bg_ref, hn_ref, s0_ref, sums_ref, mask_ref,
                   o_ref, sn_ref, S_ref, *, tb, unit, dk):
    t = pl.program_id(2)

    @pl.when(t == 0)
    def _():
        S_ref[...] = s0_ref[...]

    big, small = _gla_levels(unit)
    wg_hi, wg_lo = _split_bf16(wg_ref[...], 2)
    bg = bg_ref[...]
    hn = hn_ref[...]
    sums2 = jnp.concatenate([sums_ref[...]] * 2, axis=1)
    scale = dk ** -0.5
    row = lax.broadcasted_iota(jnp.int32, (unit, 1), 0)
    contract_last = (((1,), (1,)), ((), ()))

    for un in range(tb // unit):
        rows = slice(un * unit, (un + 1) * unit)
        gl = gl_ref[rows, :]
        logits = (jnp.dot(gl, wg_hi, preferred_element_type=F32)
                  + jnp.dot(gl, wg_lo, preferred_element_type=F32) + bg)
        gk = (jnp.minimum(logits, 0.0) - jnp.log(1.0 + jnp.exp(-jnp.abs(logits)))) * (1.0 / GLA_TAU)
        gk2 = jnp.concatenate(_split_bf16(gk, 2), axis=0)
        sm = jnp.dot(sums2, gk2, preferred_element_type=F32)
        b = sm[0:unit]
        b_end = b[unit - 1:unit]
        qs = q_ref[rows, :].astype(F32) * scale
        k = k_ref[rows, :].astype(F32)
        v = v_ref[rows, :]

        A = lax.dot_general(qs.astype(BF16), k.astype(BF16), contract_last,
                            preferred_element_type=F32) * mask_ref[0]
        for li, h in enumerate(big + small):
            upper = (row % (2 * h)) >= h
            if h in big:
                anchors = [b[g * 2 * h + h - 1:g * 2 * h + h] for g in range(unit // (2 * h))]
                anc = jnp.concatenate([jnp.broadcast_to(a, (2 * h, dk)) for a in anchors], axis=0)
                d = b - anc
                e = jnp.exp(jnp.where(upper, d, -d))
            else:
                m = 1 + small.index(h)
                e = jnp.exp(sm[m * unit:(m + 1) * unit])
            z = (jnp.where(upper, qs, k) * e).astype(BF16)
            a_l = lax.dot_general(z, z, contract_last, preferred_element_type=F32)
            A = A + a_l * mask_ref[1 + li]

        S = S_ref[...]
        o = (jnp.dot(A.astype(BF16), v, preferred_element_type=F32)
             + jnp.dot((qs * jnp.exp(b)).astype(BF16), S.astype(BF16), preferred_element_type=F32))
        k_dec_t = jnp.transpose(k * jnp.exp(b_end - b)).astype(BF16)
        decay_col = jnp.transpose(jnp.exp(b[unit - SUBLANES:]))[:, SUBLANES - 1:]
        S_ref[...] = decay_col * S + jnp.dot(k_dec_t, v, preferred_element_type=F32)
        r = r_ref[rows, :].astype(F32)
        o_ref[rows, :] = (_rms(o, hn) * (r * jax.nn.sigmoid(r))).astype(o_ref.dtype)

    @pl.when(t == pl.num_programs(2) - 1)
    def _():
        sn_ref[...] = S_ref[...]


def _gla_scan(proj, w_gate, b_gate, head_norm, s0, layer, *, seq_len, tb, name):
    _, B, H, DK, DV = s0.shape
    M = proj.shape[0]
    nt = seq_len // tb
    unit = min(LANES, tb)
    v_blk0 = (2 * H * DK) // DV
    gl_blk = (2 * H * DK + 2 * H * DV) // LANES
    sums, pair_mask = _gla_constants(unit)
    body = functools.partial(_gla_scan_body, tb=tb, unit=unit, dk=DK)
    return pl.pallas_call(
        body,
        grid=(B, H, nt),
        in_specs=[
            pl.BlockSpec((tb, DK), lambda b, h, t: (b * nt + t, h)),
            pl.BlockSpec((tb, DK), lambda b, h, t: (b * nt + t, H + h)),
            pl.BlockSpec((tb, DV), lambda b, h, t: (b * nt + t, v_blk0 + h)),
            pl.BlockSpec((tb, DV), lambda b, h, t: (b * nt + t, v_blk0 + H + h)),
            pl.BlockSpec((tb, LANES), lambda b, h, t: (b * nt + t, gl_blk)),
            pl.BlockSpec((None, LANES, DK), lambda b, h, t: (layer, 0, h)),
            pl.BlockSpec((None, 1, DK), lambda b, h, t: (layer, 0, h)),
            pl.BlockSpec((None, 1, DV), lambda b, h, t: (layer, 0, 0)),
            pl.BlockSpec((None, None, None, DK, DV), lambda b, h, t: (layer, b, h, 0, 0)),
            pl.BlockSpec(sums.shape, lambda b, h, t: (0, 0)),
            pl.BlockSpec(pair_mask.shape, lambda b, h, t: (0, 0, 0)),
        ],
        out_specs=[
            pl.BlockSpec((tb, DV), lambda b, h, t: (b * nt + t, h)),
            pl.BlockSpec((None, None, DK, DV), lambda b, h, t: (b, h, 0, 0)),
        ],
        out_shape=[
            jax.ShapeDtypeStruct((M, H * DV), BF16),
            jax.ShapeDtypeStruct((B, H, DK, DV), F32),
        ],
        scratch_shapes=[pltpu.VMEM((DK, DV), F32)],
        compiler_params=_params("arbitrary", "arbitrary", "arbitrary"),
        name=name,
    )(proj, proj, proj, proj, proj, w_gate, b_gate, head_norm, s0, sums, pair_mask)


def _band_attention_body(q_ref, k_ref, v_ref, bias_ref, o_ref, *, n_chunks, group, dh, chunk0):
    t = pl.program_id(2)
    rows_q = group * CHUNK
    bias = bias_ref[...]
    ones = jnp.ones((BAND_PAD, dh), BF16)

    def chunks(clip_start):
        for c in range(n_chunks):
            r0 = c * CHUNK
            start = pl.multiple_of((t * n_chunks + c) * CHUNK, CHUNK)
            qc = q_ref[r0:r0 + CHUNK, :]
            q4 = jnp.concatenate([qc[:, g * dh:(g + 1) * dh] for g in range(group)], axis=0)
            kb = k_ref[pl.ds(start, BAND_PAD), :]
            vb = jnp.concatenate([v_ref[pl.ds(start, BAND_PAD), :], ones], axis=1)
            s = lax.dot_general(q4, kb, (((1,), (1,)), ((), ())), preferred_element_type=F32) + bias
            if clip_start:
                first_valid = WINDOW - (chunk0 + t * n_chunks + c) * CHUNK
                col = lax.broadcasted_iota(jnp.int32, (rows_q, BAND_PAD), 1)
                s = jnp.where(col >= first_valid, s, -jnp.inf)
            m = jnp.max(s, axis=-1, keepdims=True)
            p = jnp.exp(s - m).astype(BF16)
            ov = jnp.dot(p, vb, preferred_element_type=F32)
            o4 = ov[:, :dh] / ov[:, dh:]
            for g in range(group):
                o_ref[r0:r0 + CHUNK, g * dh:(g + 1) * dh] = o4[g * CHUNK:(g + 1) * CHUNK].astype(o_ref.dtype)

    starts_early = (chunk0 + t * n_chunks) < LEFT_CHUNKS

    @pl.when(starts_early)
    def _():
        chunks(True)

    @pl.when(jnp.logical_not(starts_early))
    def _():
        chunks(False)


def _band_attention(q, kv_band, bias, layer, *, seq_len, tq, chunk0, name):
    M = q.shape[0]
    B, tp, _ = kv_band.shape
    hkv = bias.shape[1]
    group = bias.shape[2] // CHUNK
    dh = q.shape[1] // (hkv * group)
    nt = seq_len // tq
    body = functools.partial(_band_attention_body, n_chunks=tq // CHUNK, group=group, dh=dh, chunk0=chunk0)
    return pl.pallas_call(
        body,
        grid=(B, hkv, nt),
        in_specs=[
            pl.BlockSpec((tq, group * dh), lambda b, n, t: (b * nt + t, n)),
            pl.BlockSpec((None, tp, dh), lambda b, n, t: (b, 0, n)),
            pl.BlockSpec((None, tp, dh), lambda b, n, t: (b, 0, hkv + n)),
            pl.BlockSpec((None, None, group * CHUNK, BAND_PAD), lambda b, n, t: (layer, n, 0, 0)),
        ],
        out_specs=pl.BlockSpec((tq, group * dh), lambda b, n, t: (b * nt + t, n)),
        out_shape=jax.ShapeDtypeStruct(q.shape, BF16),
        compiler_params=_params("arbitrary", "arbitrary", "arbitrary"),
        name=name,
    )(q, kv_band, kv_band, bias)


def _bias_table(rel_bias, hkv):
    L, H, n_rel = rel_bias.shape
    max_rel = (n_rel - 1) // 2
    u = np.arange(BAND + CHUNK - 1)
    line = rel_bias[:, :, np.clip(WINDOW + CHUNK - 1 - u, -max_rel, max_rel) + max_rel]
    tab = jnp.stack([line[:, :, CHUNK - 1 - i:CHUNK - 1 - i + BAND] for i in range(CHUNK)], axis=2)
    tab = jnp.pad(tab.astype(F32), ((0, 0), (0, 0), (0, 0), (0, BAND_PAD - BAND)), constant_values=MASKED)
    return tab.reshape(L, hkv, (H // hkv) * CHUNK, BAND_PAD)


def _trunk(x, gla_s0, conv_s0, kv_past, wts, *, seq_len):
    M, D = x.shape
    n_seq = M // seq_len
    tm = min(512, M)
    tm_proj = min(1024, M)
    gains = wts["gains"]
    depth = gains.shape[0] // 4
    n_a = gla_s0.shape[0]
    gla_new, conv_new = [], []
    kv = None
    for l in range(depth):
        if l == n_a:
            kv = _norm_matmul(x, wts["kv_norm"], 0, wts["att_w_kv"], 0, tm=tm_proj, out_dtype=F32,
                              name="kv_proj")
            kv3 = kv.reshape(n_seq, seq_len, kv.shape[1]).astype(BF16)
            if kv_past is None:
                band = jnp.pad(kv3, ((0, 0), (WINDOW, BAND_PAD - BAND), (0, 0)))
            else:
                band = jnp.pad(jnp.concatenate([kv_past.astype(BF16), kv3], axis=1),
                               ((0, 0), (0, BAND_PAD - BAND), (0, 0)))
        if l < n_a:
            proj = _norm_matmul(x, gains, 4 * l, wts["gla_w_in"], l, tm=tm_proj, out_dtype=BF16,
                                name="gla_in_proj")
            o, s_new = _gla_scan(proj, wts["gla_w_gate"], wts["gla_b_gate"], wts["gla_head_norm"], gla_s0, l,
                                 seq_len=seq_len, tb=min(4 * CHUNK, seq_len), name="gla_scan")
            gla_new.append(s_new)
            w_o, w_layer = wts["gla_w_o"], l
        else:
            j = l - n_a
            q = _norm_matmul(x, gains, 4 * l, wts["att_w_q"], j, tm=tm_proj, out_dtype=BF16, name="att_q_proj",
                             out_scale=wts["att_scale"])
            o = _band_attention(q, band, wts["att_bias"], j, seq_len=seq_len, tq=min(4 * CHUNK, seq_len),
                                chunk0=0 if kv_past is None else LEFT_CHUNKS, name="band_attention")
            w_o, w_layer = wts["att_w_o"], j
        x = _matmul_postnorm(o, w_o, w_layer, gains, 4 * l + 1, x, tm=tm, name="mixer_out_proj")
        x, tail = _conv_ffn(x, gains, l, wts["ffn_w_up"], wts["ffn_conv_wb"], wts["ffn_w_down"], conv_s0,
                            seq_len=seq_len, tm=tm, tf=2 * MXU_DIM, tn=2 * MXU_DIM, name="conv_ffn")
        conv_new.append(tail[:, 6:8])
    return x, jnp.stack(gla_new), jnp.stack(conv_new), kv


def kernel(x_prompt, x_sample, state_gla, state_ffn_conv, cache_k, cache_v, norm_gains, gla_w_in,
           gla_w_gate, gla_b_gate, gla_head_norm, gla_w_o, kv_norm, att_w_kv, att_w_q, att_rel_bias,
           att_w_o, ffn_w_up, ffn_conv_w, ffn_conv_b, ffn_w_down):
    B, S, D = x_prompt.shape
    DB, DS, _ = x_sample.shape
    n_a, _, H, DK, DV = state_gla.shape
    depth = norm_gains.shape[0]
    hkv, dh = cache_k.shape[2], cache_k.shape[3]
    F = ffn_w_down.shape[1]
    assert cache_k.shape[1] == WINDOW and S % CHUNK == 0 and DS == CHUNK
    qkvr = 2 * H * DK + 2 * H * DV
    rank = gla_w_in.shape[2] - qkvr
    tn_in = 5 * MXU_DIM
    in_pad = -(qkvr + LANES) % tn_in + LANES - rank

    wts = dict(
        gains=norm_gains.reshape(depth * 4, 1, D),
        gla_w_in=_tile_major(jnp.pad(gla_w_in, ((0, 0), (0, 0), (0, in_pad))).astype(BF16), tn_in),
        gla_w_gate=jnp.pad(gla_w_gate, ((0, 0), (0, LANES - rank), (0, 0))),
        gla_b_gate=gla_b_gate[:, None, :],
        gla_head_norm=gla_head_norm[:, None, :],
        gla_w_o=gla_w_o.astype(BF16),
        kv_norm=kv_norm[None, None, :],
        att_w_kv=_tile_major(att_w_kv.astype(BF16)[None], 4 * MXU_DIM),
        att_w_q=_tile_major(att_w_q.astype(BF16), 4 * MXU_DIM),
        att_bias=_bias_table(att_rel_bias, hkv),
        att_scale=dh ** -0.5,
        att_w_o=att_w_o.astype(BF16),
        ffn_w_up=ffn_w_up.astype(BF16),
        ffn_conv_wb=jnp.concatenate([ffn_conv_w, ffn_conv_b[:, None, :]], axis=1),
        ffn_w_down=ffn_w_down.astype(BF16),
    )

    y_p, gla_p, conv_p, kv_p = _trunk(
        x_prompt.reshape(B * S, D), jnp.zeros((n_a, B, H, DK, DV), F32),
        jnp.zeros((depth, B, 8, F), F32), None, wts, seq_len=S)
    kv_past = jnp.concatenate([cache_k.reshape(DB, WINDOW, hkv * dh),
                               cache_v.reshape(DB, WINDOW, hkv * dh)], axis=-1)
    state8 = jnp.pad(state_ffn_conv, ((0, 0), (0, 0), (6, 0), (0, 0)))
    y_s, gla_s, conv_s, kv_s = _trunk(
        x_sample.reshape(DB * DS, D), state_gla, state8, kv_past, wts, seq_len=DS)

    keep = min(WINDOW, S)
    kv_p = kv_p.reshape(B, S, 2 * hkv * dh)[:, S - keep:].reshape(B, keep, 2, hkv, dh)
    kv_s = kv_s.reshape(DB, DS, 2, hkv, dh)
    return (y_p.reshape(B, S, D), y_s.reshape(DB, DS, D), gla_p, gla_s, conv_p, conv_s,
            kv_p[:, :, 0], kv_p[:, :, 1], kv_s[:, :, 0], kv_s[:, :, 1])
```

```python
import functools

import numpy as np
import jax
import jax.numpy as jnp
from jax import lax
from jax.experimental import pallas as pl
from jax.experimental.pallas import tpu as pltpu

CHUNK = 64
LEFT_CHUNKS = 8
WINDOW = LEFT_CHUNKS * CHUNK
BAND = WINDOW + CHUNK
LANES = 128
SUBLANES = 8
MXU_DIM = 256
BAND_PAD = ((BAND + LANES - 1) // LANES) * LANES
GLA_TAU = 16.0
EPS = 1e-6
MASKED = -1e30
VMEM_LIMIT = 56 * 1024 * 1024

F32 = jnp.float32
BF16 = jnp.bfloat16


def _params(*sem):
    return pltpu.CompilerParams(dimension_semantics=sem, vmem_limit_bytes=VMEM_LIMIT)


def _rms(x, g):
    ms = jnp.mean(x * x, axis=-1, keepdims=True)
    return x * lax.rsqrt(ms + EPS) * g


def _split_bf16(x, parts):
    out = []
    for _ in range(parts - 1):
        hi = x.astype(BF16)
        out.append(hi)
        x = x - hi.astype(F32)
    out.append(x.astype(BF16))
    return out


def _tile_major(w, tn):
    L, K, N = w.shape
    return w.reshape(L, K, N // tn, tn).transpose(0, 2, 1, 3)


def _norm_matmul_body(x_ref, g_ref, w_ref, o_ref, xn_ref, *, out_scale):
    @pl.when(pl.program_id(1) == 0)
    def _():
        xn_ref[...] = _rms(x_ref[...], g_ref[...]).astype(BF16)

    y = jnp.dot(xn_ref[...], w_ref[...], preferred_element_type=F32)
    if out_scale != 1.0:
        y = y * out_scale
    o_ref[...] = y.astype(o_ref.dtype)


def _norm_matmul(x, gains, gain_row, w, layer, *, tm, out_dtype, name, out_scale=1.0):
    M, D = x.shape
    _, nj, _, tn = w.shape
    return pl.pallas_call(
        functools.partial(_norm_matmul_body, out_scale=out_scale),
        grid=(M // tm, nj),
        in_specs=[
            pl.BlockSpec((tm, D), lambda i, j: (i, 0)),
            pl.BlockSpec((None, 1, D), lambda i, j: (gain_row, 0, 0)),
            pl.BlockSpec((None, None, D, tn), lambda i, j: (layer, j, 0, 0)),
        ],
        out_specs=pl.BlockSpec((tm, tn), lambda i, j: (i, j)),
        out_shape=jax.ShapeDtypeStruct((M, nj * tn), out_dtype),
        scratch_shapes=[pltpu.VMEM((tm, D), BF16)],
        compiler_params=_params("arbitrary", "arbitrary"),
        name=name,
    )(x, gains, w)


def _matmul_postnorm_body(a_ref, w_ref, g_ref, x_ref, o_ref):
    o_ref[...] = jnp.dot(a_ref[...], w_ref[...], preferred_element_type=F32)
    o_ref[...] = x_ref[...] + _rms(o_ref[...], g_ref[...])


def _matmul_postnorm(a, w, layer, gains, gain_row, x, *, tm, name):
    M, K = a.shape
    D = w.shape[2]
    return pl.pallas_call(
        _matmul_postnorm_body,
        grid=(M // tm,),
        in_specs=[
            pl.BlockSpec((tm, K), lambda i: (i, 0)),
            pl.BlockSpec((None, K, D), lambda i: (layer, 0, 0), pipeline_mode=pl.Buffered(1)),
            pl.BlockSpec((None, 1, D), lambda i: (gain_row, 0, 0)),
            pl.BlockSpec((tm, D), lambda i: (i, 0)),
        ],
        out_specs=pl.BlockSpec((tm, D), lambda i: (i, 0)),
        out_shape=jax.ShapeDtypeStruct((M, D), F32),
        compiler_params=_params("arbitrary"),
        name=name,
    )(a, w, gains, x)


def _ffn_up_matmul(xn_ref, wg_ref, wv_ref, gv_ref, slot):
    xn = xn_ref[...]
    gv_ref[slot, 0, SUBLANES:] = jnp.dot(xn, wg_ref[...], preferred_element_type=F32)
    gv_ref[slot, 1, SUBLANES:] = jnp.dot(xn, wv_ref[...], preferred_element_type=F32)


def _ffn_gate_tile(f, slot, gv_ref, cwb_ref, st_ref, tail_ref, carry_ref, h_ref, *, tm, tf, seg):
    if isinstance(f, int):
        cols = slice(f * tf, (f + 1) * tf)
    else:
        cols = pl.ds(pl.multiple_of(f * tf, tf), tf)
    cwb = cwb_ref[:, cols]
    cw_old, cw_mid, cw_new, cb = cwb[0:1], cwb[1:2], cwb[2:3], cwb[3:4]
    g_ref = gv_ref.at[slot, 0]
    v_ref = gv_ref.at[slot, 1]
    lead = SUBLANES
    slab = min(seg, CHUNK)
    g_ref[0:lead] = carry_ref[:, cols] if seg >= tm else st_ref[0, :, cols]
    for r in range(tm // slab):
        r0 = lead + r * slab
        g_new = g_ref[r0:r0 + slab]
        g_mid = g_ref[r0 - 1:r0 - 1 + slab]
        g_old = g_ref[r0 - 2:r0 - 2 + slab]
        if seg < tm:
            prev = st_ref[r, :, cols]
            row = lax.broadcasted_iota(jnp.int32, (slab, tf), 0)
            g_mid = jnp.where(row == 0, prev[7:8], g_mid)
            g_old = jnp.where(row == 0, prev[6:7], jnp.where(row == 1, prev[7:8], g_old))
            tail_ref[r, :, cols] = g_ref[r0 + slab - 8:r0 + slab]
        conv = cb + cw_old * g_old + cw_mid * g_mid + cw_new * g_new
        h = conv * jax.nn.sigmoid(conv) * v_ref[r0:r0 + slab]
        h_ref[r * slab:(r + 1) * slab, :] = h.astype(BF16)
    if seg >= tm:
        last = g_ref[tm:tm + lead]
        carry_ref[:, cols] = last
        tail_ref[0, :, cols] = last


def _ffn_up_body(x_ref, g2_ref, wg_ref, wv_ref, cwb_ref, st_ref, h_ref, tail_ref, xn_ref, gv_ref, carry_ref,
                 *, tm, tf, nf, seg, tiles_per_seq):
    i = pl.program_id(0)
    s = pl.program_id(1)
    gate_tile = functools.partial(_ffn_gate_tile, gv_ref=gv_ref, cwb_ref=cwb_ref, st_ref=st_ref,
                                  tail_ref=tail_ref, carry_ref=carry_ref, h_ref=h_ref, tm=tm, tf=tf, seg=seg)

    @pl.when(s == 0)
    def _():
        xn_ref[...] = _rms(x_ref[...], g2_ref[...]).astype(BF16)
        if seg >= tm:
            @pl.when(i % tiles_per_seq == 0)
            def _():
                carry_ref[...] = st_ref[0]

        _ffn_up_matmul(xn_ref, wg_ref, wv_ref, gv_ref, 0)

    for par in range(2):
        @pl.when((s >= 1) & (s < nf) & (s % 2 == par))
        def _():
            gate_tile(s - 1, 1 - par)
            _ffn_up_matmul(xn_ref, wg_ref, wv_ref, gv_ref, par)

    @pl.when(s == nf)
    def _():
        gate_tile(nf - 1, (nf - 1) % 2)


def _ffn_up(x, gains, layer, w_up, conv_wb, state8, *, seq_len, tm, tf, name):
    M, D = x.shape
    F = w_up.shape[2] // 2
    nf = F // tf
    n_seq = M // seq_len
    up = lambda s: jnp.minimum(s, nf - 1)
    done = lambda s: jnp.maximum(s - 1, 0)
    if seq_len >= tm:
        tiles_per_seq = seq_len // tm
        seq_blk = 1
        seq_of = lambda i: i // tiles_per_seq
        n_tail = M // tm
    else:
        tiles_per_seq = 1
        seq_blk = tm // seq_len
        seq_of = lambda i: i
        n_tail = n_seq
    body = functools.partial(_ffn_up_body, tm=tm, tf=tf, nf=nf, seg=seq_len, tiles_per_seq=tiles_per_seq)
    h, tails = pl.pallas_call(
        body,
        grid=(M // tm, nf + 1),
        in_specs=[
            pl.BlockSpec((tm, D), lambda i, s: (i, 0)),
            pl.BlockSpec((None, 1, D), lambda i, s: (4 * layer + 2, 0, 0)),
            pl.BlockSpec((None, D, tf), lambda i, s: (layer, 0, up(s))),
            pl.BlockSpec((None, D, tf), lambda i, s: (layer, 0, nf + up(s))),
            pl.BlockSpec((None, 4, F), lambda i, s: (layer, 0, 0)),
            pl.BlockSpec((None, seq_blk, 8, F), lambda i, s: (layer, seq_of(i), 0, 0)),
        ],
        out_specs=[
            pl.BlockSpec((tm, tf), lambda i, s: (i, done(s))),
            pl.BlockSpec((seq_blk, 8, F), lambda i, s: (i, 0, 0)),
        ],
        out_shape=[
            jax.ShapeDtypeStruct((M, F), BF16),
            jax.ShapeDtypeStruct((n_tail, 8, F), F32),
        ],
        scratch_shapes=[
            pltpu.VMEM((tm, D), BF16),
            pltpu.VMEM((2, 2, SUBLANES + tm, tf), F32),
            pltpu.VMEM((8, F), F32),
        ],
        compiler_params=_params("arbitrary", "arbitrary"),
        name=name,
    )(x, gains, w_up, w_up, conv_wb, state8)
    return h, tails[tiles_per_seq - 1::tiles_per_seq]


def _gla_levels(unit):
    halves = []
    h = unit // 2
    while h >= 1:
        halves.append(h)
        h //= 2
    return [h for h in halves if h >= SUBLANES], [h for h in halves if h < SUBLANES]


def _gla_constants(unit):
    big, small = _gla_levels(unit)
    t = np.arange(unit)[:, None]
    u = np.arange(unit)[None, :]
    blocks = [u <= t]
    masks = [t == u]
    for h in big + small:
        anchor = (t // (2 * h)) * (2 * h) + h - 1
        upper = (t % (2 * h)) >= h
        if h in small:
            blocks.append(np.where(upper, (u > anchor) & (u <= t), (u > t) & (u <= anchor)))
        same_group = (t // (2 * h)) == (u // (2 * h))
        masks.append(same_group & upper & ((u % (2 * h)) < h))
    sums = np.concatenate(blocks, axis=0).astype(np.float32)
    pair_mask = np.stack(masks).astype(np.float32)
    return jnp.asarray(sums, BF16), jnp.asarray(pair_mask, F32)


def _gla_scan_body(q_ref, k_ref, v_ref, r_ref, gl_ref, wg_ref, bg_ref, hn_ref, s0_ref, sums_ref, mask_ref,
                   o_ref, sn_ref, S_ref, *, tb, unit, dk):
    t = pl.program_id(2)

    @pl.when(t == 0)
    def _():
        S_ref[...] = s0_ref[...]

    big, small = _gla_levels(unit)
    wg_hi, wg_lo = _split_bf16(wg_ref[...], 2)
    bg = bg_ref[...]
    hn = hn_ref[...]
    sums2 = jnp.concatenate([sums_ref[...]] * 2, axis=1)
    scale = dk ** -0.5
    row = lax.broadcasted_iota(jnp.int32, (unit, 1), 0)
    contract_last = (((1,), (1,)), ((), ()))

    for un in range(tb // unit):
        rows = slice(un * unit, (un + 1) * unit)
        gl = gl_ref[rows, :]
        logits = (jnp.dot(gl, wg_hi, preferred_element_type=F32)
                  + jnp.dot(gl, wg_lo, preferred_element_type=F32) + bg)
        gk = (jnp.minimum(logits, 0.0) - jnp.log(1.0 + jnp.exp(-jnp.abs(logits)))) * (1.0 / GLA_TAU)
        gk2 = jnp.concatenate(_split_bf16(gk, 2), axis=0)
        sm = jnp.dot(sums2, gk2, preferred_element_type=F32)
        b = sm[0:unit]
        b_end = b[unit - 1:unit]
        qs = q_ref[rows, :].astype(F32) * scale
        k = k_ref[rows, :].astype(F32)
        v = v_ref[rows, :]

        A = lax.dot_general(qs.astype(BF16), k.astype(BF16), contract_last,
                            preferred_element_type=F32) * mask_ref[0]
        for li, h in enumerate(big + small):
            upper = (row % (2 * h)) >= h
            if h in big:
                anchors = [b[g * 2 * h + h - 1:g * 2 * h + h] for g in range(unit // (2 * h))]
                anc = jnp.concatenate([jnp.broadcast_to(a, (2 * h, dk)) for a in anchors], axis=0)
                d = b - anc
                e = jnp.exp(jnp.where(upper, d, -d))
            else:
                m = 1 + small.index(h)
                e = jnp.exp(sm[m * unit:(m + 1) * unit])
            z = (jnp.where(upper, qs, k) * e).astype(BF16)
            a_l = lax.dot_general(z, z, contract_last, preferred_element_type=F32)
            A = A + a_l * mask_ref[1 + li]

        S = S_ref[...]
        o = (jnp.dot(A.astype(BF16), v, preferred_element_type=F32)
             + jnp.dot((qs * jnp.exp(b)).astype(BF16), S.astype(BF16), preferred_element_type=F32))
        k_dec_t = jnp.transpose(k * jnp.exp(b_end - b)).astype(BF16)
        decay_col = jnp.transpose(jnp.exp(b[unit - SUBLANES:]))[:, SUBLANES - 1:]
        S_ref[...] = decay_col * S + jnp.dot(k_dec_t, v, preferred_element_type=F32)
        r = r_ref[rows, :].astype(F32)
        o_ref[rows, :] = (_rms(o, hn) * (r * jax.nn.sigmoid(r))).astype(o_ref.dtype)

    @pl.when(t == pl.num_programs(2) - 1)
    def _():
        sn_ref[...] = S_ref[...]


def _gla_scan(proj, w_gate, b_gate, head_norm, s0, layer, *, seq_len, tb, name):
    _, B, H, DK, DV = s0.shape
    M = proj.shape[0]
    nt = seq_len // tb
    unit = min(LANES, tb)
    v_blk0 = (2 * H * DK) // DV
    gl_blk = (2 * H * DK + 2 * H * DV) // LANES
    sums, pair_mask = _gla_constants(unit)
    body = functools.partial(_gla_scan_body, tb=tb, unit=unit, dk=DK)
    return pl.pallas_call(
        body,
        grid=(B, H, nt),
        in_specs=[
            pl.BlockSpec((tb, DK), lambda b, h, t: (b * nt + t, h)),
            pl.BlockSpec((tb, DK), lambda b, h, t: (b * nt + t, H + h)),
            pl.BlockSpec((tb, DV), lambda b, h, t: (b * nt + t, v_blk0 + h)),
            pl.BlockSpec((tb, DV), lambda b, h, t: (b * nt + t, v_blk0 + H + h)),
            pl.BlockSpec((tb, LANES), lambda b, h, t: (b * nt + t, gl_blk)),
            pl.BlockSpec((None, LANES, DK), lambda b, h, t: (layer, 0, h)),
            pl.BlockSpec((None, 1, DK), lambda b, h, t: (layer, 0, h)),
            pl.BlockSpec((None, 1, DV), lambda b, h, t: (layer, 0, 0)),
            pl.BlockSpec((None, None, None, DK, DV), lambda b, h, t: (layer, b, h, 0, 0)),
            pl.BlockSpec(sums.shape, lambda b, h, t: (0, 0)),
            pl.BlockSpec(pair_mask.shape, lambda b, h, t: (0, 0, 0)),
        ],
        out_specs=[
            pl.BlockSpec((tb, DV), lambda b, h, t: (b * nt + t, h)),
            pl.BlockSpec((None, None, DK, DV), lambda b, h, t: (b, h, 0, 0)),
        ],
        out_shape=[
            jax.ShapeDtypeStruct((M, H * DV), BF16),
            jax.ShapeDtypeStruct((B, H, DK, DV), F32),
        ],
        scratch_shapes=[pltpu.VMEM((DK, DV), F32)],
        compiler_params=_params("arbitrary", "arbitrary", "arbitrary"),
        name=name,
    )(proj, proj, proj, proj, proj, w_gate, b_gate, head_norm, s0, sums, pair_mask)


def _band_attention_body(q_ref, k_ref, v_ref, bias_ref, o_ref, *, n_chunks, group, dh, chunk0):
    t = pl.program_id(2)
    rows_q = group * CHUNK
    bias = bias_ref[...]
    ones = jnp.ones((BAND_PAD, dh), BF16)

    def chunks(clip_start):
        for c in range(n_chunks):
            r0 = c * CHUNK
            start = pl.multiple_of((t * n_chunks + c) * CHUNK, CHUNK)
            qc = q_ref[r0:r0 + CHUNK, :]
            q4 = jnp.concatenate([qc[:, g * dh:(g + 1) * dh] for g in range(group)], axis=0)
            kb = k_ref[pl.ds(start, BAND_PAD), :]
            vb = jnp.concatenate([v_ref[pl.ds(start, BAND_PAD), :], ones], axis=1)
            s = lax.dot_general(q4, kb, (((1,), (1,)), ((), ())), preferred_element_type=F32) + bias
            if clip_start:
                first_valid = WINDOW - (chunk0 + t * n_chunks + c) * CHUNK
                col = lax.broadcasted_iota(jnp.int32, (rows_q, BAND_PAD), 1)
                s = jnp.where(col >= first_valid, s, -jnp.inf)
            m = jnp.max(s, axis=-1, keepdims=True)
            p = jnp.exp(s - m).astype(BF16)
            ov = jnp.dot(p, vb, preferred_element_type=F32)
            o4 = ov[:, :dh] / ov[:, dh:]
            for g in range(group):
                o_ref[r0:r0 + CHUNK, g * dh:(g + 1) * dh] = o4[g * CHUNK:(g + 1) * CHUNK].astype(o_ref.dtype)

    starts_early = (chunk0 + t * n_chunks) < LEFT_CHUNKS

    @pl.when(starts_early)
    def _():
        chunks(True)

    @pl.when(jnp.logical_not(starts_early))
    def _():
        chunks(False)


def _band_attention(q, kv_band, bias, layer, *, seq_len, tq, chunk0, name):
    M = q.shape[0]
    B, tp, _ = kv_band.shape
    hkv = bias.shape[1]
    group = bias.shape[2] // CHUNK
    dh = q.shape[1] // (hkv * group)
    nt = seq_len // tq
    body = functools.partial(_band_attention_body, n_chunks=tq // CHUNK, group=group, dh=dh, chunk0=chunk0)
    return pl.pallas_call(
        body,
        grid=(B, hkv, nt),
        in_specs=[
            pl.BlockSpec((tq, group * dh), lambda b, n, t: (b * nt + t, n)),
            pl.BlockSpec((None, tp, dh), lambda b, n, t: (b, 0, n)),
            pl.BlockSpec((None, tp, dh), lambda b, n, t: (b, 0, hkv + n)),
            pl.BlockSpec((None, None, group * CHUNK, BAND_PAD), lambda b, n, t: (layer, n, 0, 0)),
        ],
        out_specs=pl.BlockSpec((tq, group * dh), lambda b, n, t: (b * nt + t, n)),
        out_shape=jax.ShapeDtypeStruct(q.shape, BF16),
        compiler_params=_params("arbitrary", "arbitrary", "arbitrary"),
        name=name,
    )(q, kv_band, kv_band, bias)


def _bias_table(rel_bias, hkv):
    L, H, n_rel = rel_bias.shape
    max_rel = (n_rel - 1) // 2
    u = np.arange(BAND + CHUNK - 1)
    line = rel_bias[:, :, np.clip(WINDOW + CHUNK - 1 - u, -max_rel, max_rel) + max_rel]
    tab = jnp.stack([line[:, :, CHUNK - 1 - i:CHUNK - 1 - i + BAND] for i in range(CHUNK)], axis=2)
    tab = jnp.pad(tab.astype(F32), ((0, 0), (0, 0), (0, 0), (0, BAND_PAD - BAND)), constant_values=MASKED)
    return tab.reshape(L, hkv, (H // hkv) * CHUNK, BAND_PAD)


def _trunk(x, gla_s0, conv_s0, kv_past, wts, *, seq_len):
    M, D = x.shape
    n_seq = M // seq_len
    tm = min(512, M)
    tm_proj = min(1024, M)
    gains = wts["gains"]
    depth = gains.shape[0] // 4
    n_a = gla_s0.shape[0]
    gla_new, conv_new = [], []
    kv = None
    for l in range(depth):
        if l == n_a:
            kv = _norm_matmul(x, wts["kv_norm"], 0, wts["att_w_kv"], 0, tm=tm_proj, out_dtype=F32,
                              name="kv_proj")
            kv3 = kv.reshape(n_seq, seq_len, kv.shape[1]).astype(BF16)
            if kv_past is None:
                band = jnp.pad(kv3, ((0, 0), (WINDOW, BAND_PAD - BAND), (0, 0)))
            else:
                band = jnp.pad(jnp.concatenate([kv_past.astype(BF16), kv3], axis=1),
                               ((0, 0), (0, BAND_PAD - BAND), (0, 0)))
        if l < n_a:
            proj = _norm_matmul(x, gains, 4 * l, wts["gla_w_in"], l, tm=tm_proj, out_dtype=BF16,
                                name="gla_in_proj")
            o, s_new = _gla_scan(proj, wts["gla_w_gate"], wts["gla_b_gate"], wts["gla_head_norm"], gla_s0, l,
                                 seq_len=seq_len, tb=min(4 * CHUNK, seq_len), name="gla_scan")
            gla_new.append(s_new)
            w_o, w_layer = wts["gla_w_o"], l
        else:
            j = l - n_a
            q = _norm_matmul(x, gains, 4 * l, wts["att_w_q"], j, tm=tm_proj, out_dtype=BF16, name="att_q_proj",
                             out_scale=wts["att_scale"])
            o = _band_attention(q, band, wts["att_bias"], j, seq_len=seq_len, tq=min(4 * CHUNK, seq_len),
                                chunk0=0 if kv_past is None else LEFT_CHUNKS, name="band_attention")
            w_o, w_layer = wts["att_w_o"], j
        x = _matmul_postnorm(o, w_o, w_layer, gains, 4 * l + 1, x, tm=tm, name="mixer_out_proj")
        h, tail = _ffn_up(x, gains, l, wts["ffn_w_up"], wts["ffn_conv_wb"], conv_s0, seq_len=seq_len,
                          tm=tm_proj, tf=2 * MXU_DIM, name="ffn_up")
        x = _matmul_postnorm(h, wts["ffn_w_down"], l, gains, 4 * l + 3, x, tm=tm, name="ffn_down")
        conv_new.append(tail[:, 6:8])
    return x, jnp.stack(gla_new), jnp.stack(conv_new), kv


def kernel(x_prompt, x_sample, state_gla, state_ffn_conv, cache_k, cache_v, norm_gains, gla_w_in,
           gla_w_gate, gla_b_gate, gla_head_norm, gla_w_o, kv_norm, att_w_kv, att_w_q, att_rel_bias,
           att_w_o, ffn_w_up, ffn_conv_w, ffn_conv_b, ffn_w_down):
    B, S, D = x_prompt.shape
    DB, DS, _ = x_sample.shape
    n_a, _, H, DK, DV = state_gla.shape
    depth = norm_gains.shape[0]
    hkv, dh = cache_k.shape[2], cache_k.shape[3]
    F = ffn_w_down.shape[1]
    assert cache_k.shape[1] == WINDOW and S % CHUNK == 0 and DS == CHUNK
    qkvr = 2 * H * DK + 2 * H * DV
    rank = gla_w_in.shape[2] - qkvr
    tn_in = 5 * MXU_DIM
    in_pad = -(qkvr + LANES) % tn_in + LANES - rank

    wts = dict(
        gains=norm_gains.reshape(depth * 4, 1, D),
        gla_w_in=_tile_major(jnp.pad(gla_w_in, ((0, 0), (0, 0), (0, in_pad))).astype(BF16), tn_in),
        gla_w_gate=jnp.pad(gla_w_gate, ((0, 0), (0, LANES - rank), (0, 0))),
        gla_b_gate=gla_b_gate[:, None, :],
        gla_head_norm=gla_head_norm[:, None, :],
        gla_w_o=gla_w_o.astype(BF16),
        kv_norm=kv_norm[None, None, :],
        att_w_kv=_tile_major(att_w_kv.astype(BF16)[None], 4 * MXU_DIM),
        att_w_q=att_w_q.astype(BF16)[:, None],
        att_bias=_bias_table(att_rel_bias, hkv),
        att_scale=dh ** -0.5,
        att_w_o=att_w_o.astype(BF16),
        ffn_w_up=ffn_w_up.astype(BF16),
        ffn_conv_wb=jnp.concatenate([ffn_conv_w, ffn_conv_b[:, None, :]], axis=1),
        ffn_w_down=ffn_w_down.astype(BF16),
    )

    y_p, gla_p, conv_p, kv_p = _trunk(
        x_prompt.reshape(B * S, D), jnp.zeros((n_a, B, H, DK, DV), F32),
        jnp.zeros((depth, B, 8, F), F32), None, wts, seq_len=S)
    kv_past = jnp.concatenate([cache_k.reshape(DB, WINDOW, hkv * dh),
                               cache_v.reshape(DB, WINDOW, hkv * dh)], axis=-1)
    state8 = jnp.pad(state_ffn_conv, ((0, 0), (0, 0), (6, 0), (0, 0)))
    y_s, gla_s, conv_s, kv_s = _trunk(
        x_sample.reshape(DB * DS, D), state_gla, state8, kv_past, wts, seq_len=DS)

    keep = min(WINDOW, S)
    kv_p = kv_p.reshape(B, S, 2 * hkv * dh)[:, S - keep:].reshape(B, keep, 2, hkv, dh)
    kv_s = kv_s.reshape(DB, DS, 2, hkv, dh)
    return (y_p.reshape(B, S, D), y_s.reshape(DB, DS, D), gla_p, gla_s, conv_p, conv_s,
            kv_p[:, :, 0], kv_p[:, :, 1], kv_s[:, :, 0], kv_s[:, :, 1])
```

```python
import functools

import numpy as np
import jax
import jax.numpy as jnp
from jax import lax
from jax.experimental import pallas as pl
from jax.experimental.pallas import tpu as pltpu

CHUNK = 64
LEFT_CHUNKS = 8
WINDOW = LEFT_CHUNKS * CHUNK
BAND = WINDOW + CHUNK
LANES = 128
SUBLANES = 8
MXU_DIM = 256
FFN_SLAB = 16
BAND_PAD = ((BAND + LANES - 1) // LANES) * LANES
GLA_TAU = 16.0
EPS = 1e-6
MASKED = -1e30
VMEM_LIMIT = 56 * 1024 * 1024

F32 = jnp.float32
BF16 = jnp.bfloat16


def _params(*sem):
    return pltpu.CompilerParams(dimension_semantics=sem, vmem_limit_bytes=VMEM_LIMIT)


def _rms(x, g):
    ms = jnp.mean(x * x, axis=-1, keepdims=True)
    return x * lax.rsqrt(ms + EPS) * g


def _split_bf16(x, parts):
    out = []
    for _ in range(parts - 1):
        hi = x.astype(BF16)
        out.append(hi)
        x = x - hi.astype(F32)
    out.append(x.astype(BF16))
    return out


def _tile_major(w, tn):
    L, K, N = w.shape
    return w.reshape(L, K, N // tn, tn).transpose(0, 2, 1, 3)


def _norm_matmul_body(x_ref, g_ref, w_ref, o_ref, xn_ref, *, out_scale):
    @pl.when(pl.program_id(1) == 0)
    def _():
        xn_ref[...] = _rms(x_ref[...], g_ref[...]).astype(BF16)

    y = jnp.dot(xn_ref[...], w_ref[...], preferred_element_type=F32)
    if out_scale != 1.0:
        y = y * out_scale
    o_ref[...] = y.astype(o_ref.dtype)


def _norm_matmul(x, gains, gain_row, w, layer, *, tm, out_dtype, name, out_scale=1.0):
    M, D = x.shape
    _, nj, _, tn = w.shape
    return pl.pallas_call(
        functools.partial(_norm_matmul_body, out_scale=out_scale),
        grid=(M // tm, nj),
        in_specs=[
            pl.BlockSpec((tm, D), lambda i, j: (i, 0)),
            pl.BlockSpec((None, 1, D), lambda i, j: (gain_row, 0, 0)),
            pl.BlockSpec((None, None, D, tn), lambda i, j: (layer, j, 0, 0)),
        ],
        out_specs=pl.BlockSpec((tm, tn), lambda i, j: (i, j)),
        out_shape=jax.ShapeDtypeStruct((M, nj * tn), out_dtype),
        scratch_shapes=[pltpu.VMEM((tm, D), BF16)],
        compiler_params=_params("arbitrary", "arbitrary"),
        name=name,
    )(x, gains, w)


def _matmul_postnorm_body(a_ref, w_ref, g_ref, x_ref, o_ref):
    o_ref[...] = jnp.dot(a_ref[...], w_ref[...], preferred_element_type=F32)
    o_ref[...] = x_ref[...] + _rms(o_ref[...], g_ref[...])


def _matmul_postnorm(a, w, layer, gains, gain_row, x, *, tm, name):
    M, K = a.shape
    D = w.shape[2]
    return pl.pallas_call(
        _matmul_postnorm_body,
        grid=(M // tm,),
        in_specs=[
            pl.BlockSpec((tm, K), lambda i: (i, 0)),
            pl.BlockSpec((None, K, D), lambda i: (layer, 0, 0), pipeline_mode=pl.Buffered(1)),
            pl.BlockSpec((None, 1, D), lambda i: (gain_row, 0, 0)),
            pl.BlockSpec((tm, D), lambda i: (i, 0)),
        ],
        out_specs=pl.BlockSpec((tm, D), lambda i: (i, 0)),
        out_shape=jax.ShapeDtypeStruct((M, D), F32),
        compiler_params=_params("arbitrary"),
        name=name,
    )(a, w, gains, x)


def _ffn_up_matmul(xn_ref, wg_ref, wv_ref, gv_ref, slot):
    xn = xn_ref[...]
    gv_ref[slot, 0, SUBLANES:] = jnp.dot(xn, wg_ref[...], preferred_element_type=F32)
    gv_ref[slot, 1, SUBLANES:] = jnp.dot(xn, wv_ref[...], preferred_element_type=F32)


def _ffn_gate_tile(f, slot, gv_ref, cwb_ref, st_ref, tail_ref, carry_ref, h_ref, *, tm, tf, seg):
    if isinstance(f, int):
        cols = slice(f * tf, (f + 1) * tf)
    else:
        cols = pl.ds(pl.multiple_of(f * tf, tf), tf)
    cwb = cwb_ref[:, cols]
    cw_old, cw_mid, cw_new, cb = cwb[0:1], cwb[1:2], cwb[2:3], cwb[3:4]
    g_ref = gv_ref.at[slot, 0]
    v_ref = gv_ref.at[slot, 1]
    lead = SUBLANES
    slab = FFN_SLAB
    g_ref[0:lead] = carry_ref[:, cols] if seg >= tm else st_ref[0, :, cols]
    for r in range(tm // slab):
        r0 = lead + r * slab
        g_new = g_ref[r0:r0 + slab]
        g_mid = g_ref[r0 - 1:r0 - 1 + slab]
        g_old = g_ref[r0 - 2:r0 - 2 + slab]
        if seg < tm and (r * slab) % seg == 0:
            prev = st_ref[(r * slab) // seg, :, cols]
            row = lax.broadcasted_iota(jnp.int32, (slab, tf), 0)
            g_mid = jnp.where(row == 0, prev[7:8], g_mid)
            g_old = jnp.where(row == 0, prev[6:7], jnp.where(row == 1, prev[7:8], g_old))
        if seg < tm and ((r + 1) * slab) % seg == 0:
            tail_ref[(r * slab) // seg, :, cols] = g_ref[r0 + slab - 8:r0 + slab]
        conv = cb + cw_old * g_old + cw_mid * g_mid + cw_new * g_new
        h = conv * jax.nn.sigmoid(conv) * v_ref[r0:r0 + slab]
        h_ref[r * slab:(r + 1) * slab, :] = h.astype(BF16)
    if seg >= tm:
        last = g_ref[tm:tm + lead]
        carry_ref[:, cols] = last
        tail_ref[0, :, cols] = last


def _ffn_up_body(x_ref, g2_ref, wg_ref, wv_ref, cwb_ref, st_ref, h_ref, tail_ref, xn_ref, gv_ref, carry_ref,
                 *, tm, tf, nf, seg, tiles_per_seq):
    i = pl.program_id(0)
    s = pl.program_id(1)
    gate_tile = functools.partial(_ffn_gate_tile, gv_ref=gv_ref, cwb_ref=cwb_ref, st_ref=st_ref,
                                  tail_ref=tail_ref, carry_ref=carry_ref, h_ref=h_ref, tm=tm, tf=tf, seg=seg)

    @pl.when(s == 0)
    def _():
        xn_ref[...] = _rms(x_ref[...], g2_ref[...]).astype(BF16)
        if seg >= tm:
            @pl.when(i % tiles_per_seq == 0)
            def _():
                carry_ref[...] = st_ref[0]

        _ffn_up_matmul(xn_ref, wg_ref, wv_ref, gv_ref, 0)

    for par in range(2):
        @pl.when((s >= 1) & (s < nf) & (s % 2 == par))
        def _():
            gate_tile(s - 1, 1 - par)
            _ffn_up_matmul(xn_ref, wg_ref, wv_ref, gv_ref, par)

    @pl.when(s == nf)
    def _():
        gate_tile(nf - 1, (nf - 1) % 2)


def _ffn_up(x, gains, layer, w_up, conv_wb, state8, *, seq_len, tm, tf, name):
    M, D = x.shape
    F = w_up.shape[2] // 2
    nf = F // tf
    n_seq = M // seq_len
    up = lambda s: jnp.minimum(s, nf - 1)
    done = lambda s: jnp.maximum(s - 1, 0)
    if seq_len >= tm:
        tiles_per_seq = seq_len // tm
        seq_blk = 1
        seq_of = lambda i: i // tiles_per_seq
        n_tail = M // tm
    else:
        tiles_per_seq = 1
        seq_blk = tm // seq_len
        seq_of = lambda i: i
        n_tail = n_seq
    body = functools.partial(_ffn_up_body, tm=tm, tf=tf, nf=nf, seg=seq_len, tiles_per_seq=tiles_per_seq)
    h, tails = pl.pallas_call(
        body,
        grid=(M // tm, nf + 1),
        in_specs=[
            pl.BlockSpec((tm, D), lambda i, s: (i, 0)),
            pl.BlockSpec((None, 1, D), lambda i, s: (4 * layer + 2, 0, 0)),
            pl.BlockSpec((None, D, tf), lambda i, s: (layer, 0, up(s))),
            pl.BlockSpec((None, D, tf), lambda i, s: (layer, 0, nf + up(s))),
            pl.BlockSpec((None, 4, F), lambda i, s: (layer, 0, 0)),
            pl.BlockSpec((None, seq_blk, 8, F), lambda i, s: (layer, seq_of(i), 0, 0)),
        ],
        out_specs=[
            pl.BlockSpec((tm, tf), lambda i, s: (i, done(s))),
            pl.BlockSpec((seq_blk, 8, F), lambda i, s: (i, 0, 0)),
        ],
        out_shape=[
            jax.ShapeDtypeStruct((M, F), BF16),
            jax.ShapeDtypeStruct((n_tail, 8, F), F32),
        ],
        scratch_shapes=[
            pltpu.VMEM((tm, D), BF16),
            pltpu.VMEM((2, 2, SUBLANES + tm, tf), F32),
            pltpu.VMEM((8, F), F32),
        ],
        compiler_params=_params("arbitrary", "arbitrary"),
        name=name,
    )(x, gains, w_up, w_up, conv_wb, state8)
    return h, tails[tiles_per_seq - 1::tiles_per_seq]


def _gla_levels(unit):
    halves = []
    h = unit // 2
    while h >= 1:
        halves.append(h)
        h //= 2
    return [h for h in halves if h >= SUBLANES], [h for h in halves if h < SUBLANES]


def _gla_constants(unit):
    big, small = _gla_levels(unit)
    t = np.arange(unit)[:, None]
    u = np.arange(unit)[None, :]
    blocks = [u <= t]
    masks = [t == u]
    for h in big + small:
        anchor = (t // (2 * h)) * (2 * h) + h - 1
        upper = (t % (2 * h)) >= h
        if h in small:
            blocks.append(np.where(upper, (u > anchor) & (u <= t), (u > t) & (u <= anchor)))
        same_group = (t // (2 * h)) == (u // (2 * h))
        masks.append(same_group & upper & ((u % (2 * h)) < h))
    sums = np.concatenate(blocks, axis=0).astype(np.float32)
    pair_mask = np.stack(masks).astype(np.float32)
    return jnp.asarray(sums, BF16), jnp.asarray(pair_mask, F32)


def _gla_scan_body(q_ref, k_ref, v_ref, r_ref, gl_ref, wg_ref, bg_ref, hn_ref, s0_ref, sums_ref, mask_ref,
                   o_ref, sn_ref, S_ref, *, tb, unit, dk):
    t = pl.program_id(2)

    @pl.when(t == 0)
    def _():
        S_ref[...] = s0_ref[...]

    big, small = _gla_levels(unit)
    wg_hi, wg_lo = _split_bf16(wg_ref[...], 2)
    bg = bg_ref[...]
    hn = hn_ref[...]
    sums2 = jnp.concatenate([sums_ref[...]] * 2, axis=1)
    scale = dk ** -0.5
    row = lax.broadcasted_iota(jnp.int32, (unit, 1), 0)
    contract_last = (((1,), (1,)), ((), ()))

    for un in range(tb // unit):
        rows = slice(un * unit, (un + 1) * unit)
        gl = gl_ref[rows, :]
        logits = (jnp.dot(gl, wg_hi, preferred_element_type=F32)
                  + jnp.dot(gl, wg_lo, preferred_element_type=F32) + bg)
        gk = (jnp.minimum(logits, 0.0) - jnp.log(1.0 + jnp.exp(-jnp.abs(logits)))) * (1.0 / GLA_TAU)
        gk2 = jnp.concatenate(_split_bf16(gk, 2), axis=0)
        sm = jnp.dot(sums2, gk2, preferred_element_type=F32)
        b = sm[0:unit]
        b_end = b[unit - 1:unit]
        qs = q_ref[rows, :].astype(F32) * scale
        k = k_ref[rows, :].astype(F32)
        v = v_ref[rows, :]

        A = lax.dot_general(qs.astype(BF16), k.astype(BF16), contract_last,
                            preferred_element_type=F32) * mask_ref[0]
        for li, h in enumerate(big + small):
            upper = (row % (2 * h)) >= h
            if h in big:
                anchors = [b[g * 2 * h + h - 1:g * 2 * h + h] for g in range(unit // (2 * h))]
                anc = jnp.concatenate([jnp.broadcast_to(a, (2 * h, dk)) for a in anchors], axis=0)
                d = b - anc
                e = jnp.exp(jnp.where(upper, d, -d))
            else:
                m = 1 + small.index(h)
                e = jnp.exp(sm[m * unit:(m + 1) * unit])
            z = (jnp.where(upper, qs, k) * e).astype(BF16)
            a_l = lax.dot_general(z, z, contract_last, preferred_element_type=F32)
            A = A + a_l * mask_ref[1 + li]

        S = S_ref[...]
        o = (jnp.dot(A.astype(BF16), v, preferred_element_type=F32)
             + jnp.dot((qs * jnp.exp(b)).astype(BF16), S.astype(BF16), preferred_element_type=F32))
        k_dec_t = jnp.transpose(k * jnp.exp(b_end - b)).astype(BF16)
        decay_col = jnp.transpose(jnp.exp(b[unit - SUBLANES:]))[:, SUBLANES - 1:]
        S_ref[...] = decay_col * S + jnp.dot(k_dec_t, v, preferred_element_type=F32)
        r = r_ref[rows, :].astype(F32)
        o_ref[rows, :] = (_rms(o, hn) * (r * jax.nn.sigmoid(r))).astype(o_ref.dtype)

    @pl.when(t == pl.num_programs(2) - 1)
    def _():
        sn_ref[...] = S_ref[...]


def _gla_scan(proj, w_gate, b_gate, head_norm, s0, layer, *, seq_len, tb, name):
    _, B, H, DK, DV = s0.shape
    M = proj.shape[0]
    nt = seq_len // tb
    unit = min(LANES, tb)
    v_blk0 = (2 * H * DK) // DV
    gl_blk = (2 * H * DK + 2 * H * DV) // LANES
    sums, pair_mask = _gla_constants(unit)
    body = functools.partial(_gla_scan_body, tb=tb, unit=unit, dk=DK)
    return pl.pallas_call(
        body,
        grid=(B, H, nt),
        in_specs=[
            pl.BlockSpec((tb, DK), lambda b, h, t: (b * nt + t, h)),
            pl.BlockSpec((tb, DK), lambda b, h, t: (b * nt + t, H + h)),
            pl.BlockSpec((tb, DV), lambda b, h, t: (b * nt + t, v_blk0 + h)),
            pl.BlockSpec((tb, DV), lambda b, h, t: (b * nt + t, v_blk0 + H + h)),
            pl.BlockSpec((tb, LANES), lambda b, h, t: (b * nt + t, gl_blk)),
            pl.BlockSpec((None, LANES, DK), lambda b, h, t: (layer, 0, h)),
            pl.BlockSpec((None, 1, DK), lambda b, h, t: (layer, 0, h)),
            pl.BlockSpec((None, 1, DV), lambda b, h, t: (layer, 0, 0)),
            pl.BlockSpec((None, None, None, DK, DV), lambda b, h, t: (layer, b, h, 0, 0)),
            pl.BlockSpec(sums.shape, lambda b, h, t: (0, 0)),
            pl.BlockSpec(pair_mask.shape, lambda b, h, t: (0, 0, 0)),
        ],
        out_specs=[
            pl.BlockSpec((tb, DV), lambda b, h, t: (b * nt + t, h)),
            pl.BlockSpec((None, None, DK, DV), lambda b, h, t: (b, h, 0, 0)),
        ],
        out_shape=[
            jax.ShapeDtypeStruct((M, H * DV), BF16),
            jax.ShapeDtypeStruct((B, H, DK, DV), F32),
        ],
        scratch_shapes=[pltpu.VMEM((DK, DV), F32)],
        compiler_params=_params("arbitrary", "arbitrary", "arbitrary"),
        name=name,
    )(proj, proj, proj, proj, proj, w_gate, b_gate, head_norm, s0, sums, pair_mask)


def _band_attention_body(q_ref, k_ref, v_ref, bias_ref, o_ref, *, n_chunks, group, dh, chunk0):
    t = pl.program_id(2)
    rows_q = group * CHUNK
    bias = bias_ref[...]
    ones = jnp.ones((BAND_PAD, dh), BF16)

    def chunks(clip_start):
        for c in range(n_chunks):
            r0 = c * CHUNK
            start = pl.multiple_of((t * n_chunks + c) * CHUNK, CHUNK)
            qc = q_ref[r0:r0 + CHUNK, :]
            q4 = jnp.concatenate([qc[:, g * dh:(g + 1) * dh] for g in range(group)], axis=0)
            kb = k_ref[pl.ds(start, BAND_PAD), :]
            vb = jnp.concatenate([v_ref[pl.ds(start, BAND_PAD), :], ones], axis=1)
            s = lax.dot_general(q4, kb, (((1,), (1,)), ((), ())), preferred_element_type=F32) + bias
            if clip_start:
                first_valid = WINDOW - (chunk0 + t * n_chunks + c) * CHUNK
                col = lax.broadcasted_iota(jnp.int32, (rows_q, BAND_PAD), 1)
                s = jnp.where(col >= first_valid, s, -jnp.inf)
            m = jnp.max(s, axis=-1, keepdims=True)
            p = jnp.exp(s - m).astype(BF16)
            ov = jnp.dot(p, vb, preferred_element_type=F32)
            o4 = ov[:, :dh] / ov[:, dh:]
            for g in range(group):
                o_ref[r0:r0 + CHUNK, g * dh:(g + 1) * dh] = o4[g * CHUNK:(g + 1) * CHUNK].astype(o_ref.dtype)

    starts_early = (chunk0 + t * n_chunks) < LEFT_CHUNKS

    @pl.when(starts_early)
    def _():
        chunks(True)

    @pl.when(jnp.logical_not(starts_early))
    def _():
        chunks(False)


def _band_attention(q, kv_band, bias, layer, *, seq_len, tq, chunk0, name):
    M = q.shape[0]
    B, tp, _ = kv_band.shape
    hkv = bias.shape[1]
    group = bias.shape[2] // CHUNK
    dh = q.shape[1] // (hkv * group)
    nt = seq_len // tq
    body = functools.partial(_band_attention_body, n_chunks=tq // CHUNK, group=group, dh=dh, chunk0=chunk0)
    return pl.pallas_call(
        body,
        grid=(B, hkv, nt),
        in_specs=[
            pl.BlockSpec((tq, group * dh), lambda b, n, t: (b * nt + t, n)),
            pl.BlockSpec((None, tp, dh), lambda b, n, t: (b, 0, n)),
            pl.BlockSpec((None, tp, dh), lambda b, n, t: (b, 0, hkv + n)),
            pl.BlockSpec((None, None, group * CHUNK, BAND_PAD), lambda b, n, t: (layer, n, 0, 0)),
        ],
        out_specs=pl.BlockSpec((tq, group * dh), lambda b, n, t: (b * nt + t, n)),
        out_shape=jax.ShapeDtypeStruct(q.shape, BF16),
        compiler_params=_params("arbitrary", "arbitrary", "arbitrary"),
        name=name,
    )(q, kv_band, kv_band, bias)


def _bias_table(rel_bias, hkv):
    L, H, n_rel = rel_bias.shape
    max_rel = (n_rel - 1) // 2
    u = np.arange(BAND + CHUNK - 1)
    line = rel_bias[:, :, np.clip(WINDOW + CHUNK - 1 - u, -max_rel, max_rel) + max_rel]
    tab = jnp.stack([line[:, :, CHUNK - 1 - i:CHUNK - 1 - i + BAND] for i in range(CHUNK)], axis=2)
    tab = jnp.pad(tab.astype(F32), ((0, 0), (0, 0), (0, 0), (0, BAND_PAD - BAND)), constant_values=MASKED)
    return tab.reshape(L, hkv, (H // hkv) * CHUNK, BAND_PAD)


def _trunk(x, gla_s0, conv_s0, kv_past, wts, *, seq_len):
    M, D = x.shape
    n_seq = M // seq_len
    tm = min(512, M)
    tm_proj = min(1024, M)
    gains = wts["gains"]
    depth = gains.shape[0] // 4
    n_a = gla_s0.shape[0]
    gla_new, conv_new = [], []
    kv = None
    for l in range(depth):
        if l == n_a:
            kv = _norm_matmul(x, wts["kv_norm"], 0, wts["att_w_kv"], 0, tm=tm_proj, out_dtype=F32,
                              name="kv_proj")
            kv3 = kv.reshape(n_seq, seq_len, kv.shape[1]).astype(BF16)
            if kv_past is None:
                band = jnp.pad(kv3, ((0, 0), (WINDOW, BAND_PAD - BAND), (0, 0)))
            else:
                band = jnp.pad(jnp.concatenate([kv_past.astype(BF16), kv3], axis=1),
                               ((0, 0), (0, BAND_PAD - BAND), (0, 0)))
        if l < n_a:
            proj = _norm_matmul(x, gains, 4 * l, wts["gla_w_in"], l, tm=tm_proj, out_dtype=BF16,
                                name="gla_in_proj")
            o, s_new = _gla_scan(proj, wts["gla_w_gate"], wts["gla_b_gate"], wts["gla_head_norm"], gla_s0, l,
                                 seq_len=seq_len, tb=min(4 * CHUNK, seq_len), name="gla_scan")
            gla_new.append(s_new)
            w_o, w_layer = wts["gla_w_o"], l
        else:
            j = l - n_a
            q = _norm_matmul(x, gains, 4 * l, wts["att_w_q"], j, tm=tm_proj, out_dtype=BF16, name="att_q_proj",
                             out_scale=wts["att_scale"])
            o = _band_attention(q, band, wts["att_bias"], j, seq_len=seq_len, tq=min(4 * CHUNK, seq_len),
                                chunk0=0 if kv_past is None else LEFT_CHUNKS, name="band_attention")
            w_o, w_layer = wts["att_w_o"], j
        x = _matmul_postnorm(o, w_o, w_layer, gains, 4 * l + 1, x, tm=tm, name="mixer_out_proj")
        h, tail = _ffn_up(x, gains, l, wts["ffn_w_up"], wts["ffn_conv_wb"], conv_s0, seq_len=seq_len,
                          tm=tm_proj, tf=2 * MXU_DIM, name="ffn_up")
        x = _matmul_postnorm(h, wts["ffn_w_down"], l, gains, 4 * l + 3, x, tm=tm, name="ffn_down")
        conv_new.append(tail[:, 6:8])
    return x, jnp.stack(gla_new), jnp.stack(conv_new), kv


def kernel(x_prompt, x_sample, state_gla, state_ffn_conv, cache_k, cache_v, norm_gains, gla_w_in,
           gla_w_gate, gla_b_gate, gla_head_norm, gla_w_o, kv_norm, att_w_kv, att_w_q, att_rel_bias,
           att_w_o, ffn_w_up, ffn_conv_w, ffn_conv_b, ffn_w_down):
    B, S, D = x_prompt.shape
    DB, DS, _ = x_sample.shape
    n_a, _, H, DK, DV = state_gla.shape
    depth = norm_gains.shape[0]
    hkv, dh = cache_k.shape[2], cache_k.shape[3]
    F = ffn_w_down.shape[1]
    assert cache_k.shape[1] == WINDOW and S % CHUNK == 0 and DS == CHUNK
    qkvr = 2 * H * DK + 2 * H * DV
    rank = gla_w_in.shape[2] - qkvr
    tn_in = 5 * MXU_DIM
    in_pad = -(qkvr + LANES) % tn_in + LANES - rank

    wts = dict(
        gains=norm_gains.reshape(depth * 4, 1, D),
        gla_w_in=_tile_major(jnp.pad(gla_w_in, ((0, 0), (0, 0), (0, in_pad))).astype(BF16), tn_in),
        gla_w_gate=jnp.pad(gla_w_gate, ((0, 0), (0, LANES - rank), (0, 0))),
        gla_b_gate=gla_b_gate[:, None, :],
        gla_head_norm=gla_head_norm[:, None, :],
        gla_w_o=gla_w_o.astype(BF16),
        kv_norm=kv_norm[None, None, :],
        att_w_kv=_tile_major(att_w_kv.astype(BF16)[None], 4 * MXU_DIM),
        att_w_q=att_w_q.astype(BF16)[:, None],
        att_bias=_bias_table(att_rel_bias, hkv),
        att_scale=dh ** -0.5,
        att_w_o=att_w_o.astype(BF16),
        ffn_w_up=ffn_w_up.astype(BF16),
        ffn_conv_wb=jnp.concatenate([ffn_conv_w, ffn_conv_b[:, None, :]], axis=1),
        ffn_w_down=ffn_w_down.astype(BF16),
    )

    y_p, gla_p, conv_p, kv_p = _trunk(
        x_prompt.reshape(B * S, D), jnp.zeros((n_a, B, H, DK, DV), F32),
        jnp.zeros((depth, B, 8, F), F32), None, wts, seq_len=S)
    kv_past = jnp.concatenate([cache_k.reshape(DB, WINDOW, hkv * dh),
                               cache_v.reshape(DB, WINDOW, hkv * dh)], axis=-1)
    state8 = jnp.pad(state_ffn_conv, ((0, 0), (0, 0), (6, 0), (0, 0)))
    y_s, gla_s, conv_s, kv_s = _trunk(
        x_sample.reshape(DB * DS, D), state_gla, state8, kv_past, wts, seq_len=DS)

    keep = min(WINDOW, S)
    kv_p = kv_p.reshape(B, S, 2 * hkv * dh)[:, S - keep:].reshape(B, keep, 2, hkv, dh)
    kv_s = kv_s.reshape(DB, DS, 2, hkv, dh)
    return (y_p.reshape(B, S, D), y_s.reshape(DB, DS, D), gla_p, gla_s, conv_p, conv_s,
            kv_p[:, :, 0], kv_p[:, :, 1], kv_s[:, :, 0], kv_s[:, :, 1])
```

```python
import functools

import numpy as np
import jax
import jax.numpy as jnp
from jax import lax
from jax.experimental import pallas as pl
from jax.experimental.pallas import tpu as pltpu

CHUNK = 64
LEFT_CHUNKS = 8
WINDOW = LEFT_CHUNKS * CHUNK
BAND = WINDOW + CHUNK
LANES = 128
SUBLANES = 8
MXU_DIM = 256
FFN_SLAB = 64
BAND_PAD = ((BAND + LANES - 1) // LANES) * LANES
GLA_TAU = 16.0
EPS = 1e-6
MASKED = -1e30
VMEM_LIMIT = 56 * 1024 * 1024

F32 = jnp.float32
BF16 = jnp.bfloat16


def _params(*sem):
    return pltpu.CompilerParams(dimension_semantics=sem, vmem_limit_bytes=VMEM_LIMIT)


def _rms(x, g):
    ms = jnp.mean(x * x, axis=-1, keepdims=True)
    return x * lax.rsqrt(ms + EPS) * g


def _split_bf16(x, parts):
    out = []
    for _ in range(parts - 1):
        hi = x.astype(BF16)
        out.append(hi)
        x = x - hi.astype(F32)
    out.append(x.astype(BF16))
    return out


def _tile_major(w, tn):
    L, K, N = w.shape
    return w.reshape(L, K, N // tn, tn).transpose(0, 2, 1, 3)


def _norm_matmul_body(x_ref, g_ref, w_ref, o_ref, xn_ref, *, out_scale):
    @pl.when(pl.program_id(1) == 0)
    def _():
        xn_ref[...] = _rms(x_ref[...], g_ref[...]).astype(BF16)

    y = jnp.dot(xn_ref[...], w_ref[...], preferred_element_type=F32)
    if out_scale != 1.0:
        y = y * out_scale
    o_ref[...] = y.astype(o_ref.dtype)


def _norm_matmul(x, gains, gain_row, w, layer, *, tm, out_dtype, name, out_scale=1.0):
    M, D = x.shape
    _, nj, _, tn = w.shape
    return pl.pallas_call(
        functools.partial(_norm_matmul_body, out_scale=out_scale),
        grid=(M // tm, nj),
        in_specs=[
            pl.BlockSpec((tm, D), lambda i, j: (i, 0)),
            pl.BlockSpec((None, 1, D), lambda i, j: (gain_row, 0, 0)),
            pl.BlockSpec((None, None, D, tn), lambda i, j: (layer, j, 0, 0)),
        ],
        out_specs=pl.BlockSpec((tm, tn), lambda i, j: (i, j)),
        out_shape=jax.ShapeDtypeStruct((M, nj * tn), out_dtype),
        scratch_shapes=[pltpu.VMEM((tm, D), BF16)],
        compiler_params=_params("arbitrary", "arbitrary"),
        name=name,
    )(x, gains, w)


def _matmul_postnorm_body(a_ref, w_ref, g_ref, x_ref, o_ref):
    o_ref[...] = jnp.dot(a_ref[...], w_ref[...], preferred_element_type=F32)
    o_ref[...] = x_ref[...] + _rms(o_ref[...], g_ref[...])


def _matmul_postnorm(a, w, layer, gains, gain_row, x, *, tm, name):
    M, K = a.shape
    D = w.shape[2]
    return pl.pallas_call(
        _matmul_postnorm_body,
        grid=(M // tm,),
        in_specs=[
            pl.BlockSpec((tm, K), lambda i: (i, 0)),
            pl.BlockSpec((None, K, D), lambda i: (layer, 0, 0), pipeline_mode=pl.Buffered(1)),
            pl.BlockSpec((None, 1, D), lambda i: (gain_row, 0, 0)),
            pl.BlockSpec((tm, D), lambda i: (i, 0)),
        ],
        out_specs=pl.BlockSpec((tm, D), lambda i: (i, 0)),
        out_shape=jax.ShapeDtypeStruct((M, D), F32),
        compiler_params=_params("arbitrary"),
        name=name,
    )(a, w, gains, x)


def _ffn_up_matmul(xn_ref, wg_ref, wv_ref, gv_ref, slot):
    xn = xn_ref[...]
    gv_ref[slot, 0, SUBLANES:] = jnp.dot(xn, wg_ref[...], preferred_element_type=F32)
    gv_ref[slot, 1, SUBLANES:] = jnp.dot(xn, wv_ref[...], preferred_element_type=F32)


def _ffn_gate_tile(f, slot, gv_ref, cwb_ref, st_ref, tail_ref, carry_ref, h_ref, *, tm, tf, seg):
    if isinstance(f, int):
        cols = slice(f * tf, (f + 1) * tf)
    else:
        cols = pl.ds(pl.multiple_of(f * tf, tf), tf)
    cwb = cwb_ref[:, cols]
    cw_old, cw_mid, cw_new, cb = cwb[0:1], cwb[1:2], cwb[2:3], cwb[3:4]
    g_ref = gv_ref.at[slot, 0]
    v_ref = gv_ref.at[slot, 1]
    lead = SUBLANES
    slab = FFN_SLAB
    g_ref[0:lead] = carry_ref[:, cols] if seg >= tm else st_ref[0, :, cols]
    for r in range(tm // slab):
        r0 = lead + r * slab
        g_new = g_ref[r0:r0 + slab]
        g_mid = g_ref[r0 - 1:r0 - 1 + slab]
        g_old = g_ref[r0 - 2:r0 - 2 + slab]
        if seg < tm and (r * slab) % seg == 0:
            prev = st_ref[(r * slab) // seg, :, cols]
            row = lax.broadcasted_iota(jnp.int32, (slab, tf), 0)
            g_mid = jnp.where(row == 0, prev[7:8], g_mid)
            g_old = jnp.where(row == 0, prev[6:7], jnp.where(row == 1, prev[7:8], g_old))
        if seg < tm and ((r + 1) * slab) % seg == 0:
            tail_ref[(r * slab) // seg, :, cols] = g_ref[r0 + slab - 8:r0 + slab]
        conv = cb + cw_old * g_old + cw_mid * g_mid + cw_new * g_new
        h = conv * jax.nn.sigmoid(conv) * v_ref[r0:r0 + slab]
        h_ref[r * slab:(r + 1) * slab, :] = h.astype(BF16)
    if seg >= tm:
        last = g_ref[tm:tm + lead]
        carry_ref[:, cols] = last
        tail_ref[0, :, cols] = last


def _ffn_up_body(x_ref, g2_ref, wg_ref, wv_ref, cwb_ref, st_ref, h_ref, tail_ref, xn_ref, gv_ref, carry_ref,
                 *, tm, tf, nf, seg, tiles_per_seq):
    i = pl.program_id(0)
    s = pl.program_id(1)
    gate_tile = functools.partial(_ffn_gate_tile, gv_ref=gv_ref, cwb_ref=cwb_ref, st_ref=st_ref,
                                  tail_ref=tail_ref, carry_ref=carry_ref, h_ref=h_ref, tm=tm, tf=tf, seg=seg)

    @pl.when(s == 0)
    def _():
        xn_ref[...] = _rms(x_ref[...], g2_ref[...]).astype(BF16)
        if seg >= tm:
            @pl.when(i % tiles_per_seq == 0)
            def _():
                carry_ref[...] = st_ref[0]

        _ffn_up_matmul(xn_ref, wg_ref, wv_ref, gv_ref, 0)

    for par in range(2):
        @pl.when((s >= 1) & (s < nf) & (s % 2 == par))
        def _():
            gate_tile(s - 1, 1 - par)
            _ffn_up_matmul(xn_ref, wg_ref, wv_ref, gv_ref, par)

    @pl.when(s == nf)
    def _():
        gate_tile(nf - 1, (nf - 1) % 2)


def _ffn_up(x, gains, layer, w_up, conv_wb, state8, *, seq_len, tm, tf, name):
    M, D = x.shape
    F = w_up.shape[2] // 2
    nf = F // tf
    n_seq = M // seq_len
    up = lambda s: jnp.minimum(s, nf - 1)
    done = lambda s: jnp.maximum(s - 1, 0)
    if seq_len >= tm:
        tiles_per_seq = seq_len // tm
        seq_blk = 1
        seq_of = lambda i: i // tiles_per_seq
        n_tail = M // tm
    else:
        tiles_per_seq = 1
        seq_blk = tm // seq_len
        seq_of = lambda i: i
        n_tail = n_seq
    body = functools.partial(_ffn_up_body, tm=tm, tf=tf, nf=nf, seg=seq_len, tiles_per_seq=tiles_per_seq)
    h, tails = pl.pallas_call(
        body,
        grid=(M // tm, nf + 1),
        in_specs=[
            pl.BlockSpec((tm, D), lambda i, s: (i, 0)),
            pl.BlockSpec((None, 1, D), lambda i, s: (4 * layer + 2, 0, 0)),
            pl.BlockSpec((None, D, tf), lambda i, s: (layer, 0, up(s))),
            pl.BlockSpec((None, D, tf), lambda i, s: (layer, 0, nf + up(s))),
            pl.BlockSpec((None, 4, F), lambda i, s: (layer, 0, 0)),
            pl.BlockSpec((None, seq_blk, 8, F), lambda i, s: (layer, seq_of(i), 0, 0)),
        ],
        out_specs=[
            pl.BlockSpec((tm, tf), lambda i, s: (i, done(s))),
            pl.BlockSpec((seq_blk, 8, F), lambda i, s: (i, 0, 0)),
        ],
        out_shape=[
            jax.ShapeDtypeStruct((M, F), BF16),
            jax.ShapeDtypeStruct((n_tail, 8, F), F32),
        ],
        scratch_shapes=[
            pltpu.VMEM((tm, D), BF16),
            pltpu.VMEM((2, 2, SUBLANES + tm, tf), F32),
            pltpu.VMEM((8, F), F32),
        ],
        compiler_params=_params("arbitrary", "arbitrary"),
        name=name,
    )(x, gains, w_up, w_up, conv_wb, state8)
    return h, tails[tiles_per_seq - 1::tiles_per_seq]


def _gla_levels(unit):
    halves = []
    h = unit // 2
    while h >= 1:
        halves.append(h)
        h //= 2
    return [h for h in halves if h >= SUBLANES], [h for h in halves if h < SUBLANES]


def _gla_constants(unit):
    big, small = _gla_levels(unit)
    t = np.arange(unit)[:, None]
    u = np.arange(unit)[None, :]
    blocks = [u <= t]
    masks = [t == u]
    for h in big + small:
        anchor = (t // (2 * h)) * (2 * h) + h - 1
        upper = (t % (2 * h)) >= h
        if h in small:
            blocks.append(np.where(upper, (u > anchor) & (u <= t), (u > t) & (u <= anchor)))
        same_group = (t // (2 * h)) == (u // (2 * h))
        masks.append(same_group & upper & ((u % (2 * h)) < h))
    sums = np.concatenate(blocks, axis=0).astype(np.float32)
    pair_mask = np.stack(masks).astype(np.float32)
    return jnp.asarray(sums, BF16), jnp.asarray(pair_mask, F32)


def _gla_scan_body(q_ref, k_ref, v_ref, r_ref, gl_ref, wg_ref, bg_ref, hn_ref, s0_ref, sums_ref, mask_ref,
                   o_ref, sn_ref, S_ref, *, tb, unit, dk):
    t = pl.program_id(2)

    @pl.when(t == 0)
    def _():
        S_ref[...] = s0_ref[...]

    big, small = _gla_levels(unit)
    wg_hi, wg_lo = _split_bf16(wg_ref[...], 2)
    bg = bg_ref[...]
    hn = hn_ref[...]
    sums2 = jnp.concatenate([sums_ref[...]] * 2, axis=1)
    scale = dk ** -0.5
    row = lax.broadcasted_iota(jnp.int32, (unit, 1), 0)
    contract_last = (((1,), (1,)), ((), ()))

    for un in range(tb // unit):
        rows = slice(un * unit, (un + 1) * unit)
        gl = gl_ref[rows, :]
        logits = (jnp.dot(gl, wg_hi, preferred_element_type=F32)
                  + jnp.dot(gl, wg_lo, preferred_element_type=F32) + bg)
        gk = (jnp.minimum(logits, 0.0) - jnp.log(1.0 + jnp.exp(-jnp.abs(logits)))) * (1.0 / GLA_TAU)
        gk2 = jnp.concatenate(_split_bf16(gk, 2), axis=0)
        sm = jnp.dot(sums2, gk2, preferred_element_type=F32)
        b = sm[0:unit]
        b_end = b[unit - 1:unit]
        qs = q_ref[rows, :].astype(F32) * scale
        k = k_ref[rows, :].astype(F32)
        v = v_ref[rows, :]

        A = lax.dot_general(qs.astype(BF16), k.astype(BF16), contract_last,
                            preferred_element_type=F32) * mask_ref[0]
        for li, h in enumerate(big + small):
            upper = (row % (2 * h)) >= h
            if h in big:
                anchors = [b[g * 2 * h + h - 1:g * 2 * h + h] for g in range(unit // (2 * h))]
                anc = jnp.concatenate([jnp.broadcast_to(a, (2 * h, dk)) for a in anchors], axis=0)
                d = b - anc
                e = jnp.exp(jnp.where(upper, d, -d))
            else:
                m = 1 + small.index(h)
                e = jnp.exp(sm[m * unit:(m + 1) * unit])
            z = (jnp.where(upper, qs, k) * e).astype(BF16)
            a_l = lax.dot_general(z, z, contract_last, preferred_element_type=F32)
            A = A + a_l * mask_ref[1 + li]

        S = S_ref[...]
        o = (jnp.dot(A.astype(BF16), v, preferred_element_type=F32)
             + jnp.dot((qs * jnp.exp(b)).astype(BF16), S.astype(BF16), preferred_element_type=F32))
        k_dec_t = jnp.transpose(k * jnp.exp(b_end - b)).astype(BF16)
        decay_col = jnp.transpose(jnp.exp(b[unit - SUBLANES:]))[:, SUBLANES - 1:]
        S_ref[...] = decay_col * S + jnp.dot(k_dec_t, v, preferred_element_type=F32)
        r = r_ref[rows, :].astype(F32)
        o_ref[rows, :] = (_rms(o, hn) * (r * jax.nn.sigmoid(r))).astype(o_ref.dtype)

    @pl.when(t == pl.num_programs(2) - 1)
    def _():
        sn_ref[...] = S_ref[...]


def _gla_scan(proj, w_gate, b_gate, head_norm, s0, layer, *, seq_len, tb, name):
    _, B, H, DK, DV = s0.shape
    M = proj.shape[0]
    nt = seq_len // tb
    unit = min(LANES, tb)
    v_blk0 = (2 * H * DK) // DV
    gl_blk = (2 * H * DK + 2 * H * DV) // LANES
    sums, pair_mask = _gla_constants(unit)
    body = functools.partial(_gla_scan_body, tb=tb, unit=unit, dk=DK)
    return pl.pallas_call(
        body,
        grid=(B, H, nt),
        in_specs=[
            pl.BlockSpec((tb, DK), lambda b, h, t: (b * nt + t, h)),
            pl.BlockSpec((tb, DK), lambda b, h, t: (b * nt + t, H + h)),
            pl.BlockSpec((tb, DV), lambda b, h, t: (b * nt + t, v_blk0 + h)),
            pl.BlockSpec((tb, DV), lambda b, h, t: (b * nt + t, v_blk0 + H + h)),
            pl.BlockSpec((tb, LANES), lambda b, h, t: (b * nt + t, gl_blk)),
            pl.BlockSpec((None, LANES, DK), lambda b, h, t: (layer, 0, h)),
            pl.BlockSpec((None, 1, DK), lambda b, h, t: (layer, 0, h)),
            pl.BlockSpec((None, 1, DV), lambda b, h, t: (layer, 0, 0)),
            pl.BlockSpec((None, None, None, DK, DV), lambda b, h, t: (layer, b, h, 0, 0)),
            pl.BlockSpec(sums.shape, lambda b, h, t: (0, 0)),
            pl.BlockSpec(pair_mask.shape, lambda b, h, t: (0, 0, 0)),
        ],
        out_specs=[
            pl.BlockSpec((tb, DV), lambda b, h, t: (b * nt + t, h)),
            pl.BlockSpec((None, None, DK, DV), lambda b, h, t: (b, h, 0, 0)),
        ],
        out_shape=[
            jax.ShapeDtypeStruct((M, H * DV), BF16),
            jax.ShapeDtypeStruct((B, H, DK, DV), F32),
        ],
        scratch_shapes=[pltpu.VMEM((DK, DV), F32)],
        compiler_params=_params("arbitrary", "arbitrary", "arbitrary"),
        name=name,
    )(proj, proj, proj, proj, proj, w_gate, b_gate, head_norm, s0, sums, pair_mask)


def _band_attention_body(q_ref, k_ref, v_ref, bias_ref, o_ref, *, n_chunks, group, dh, chunk0):
    t = pl.program_id(2)
    rows_q = group * CHUNK
    bias = bias_ref[...]
    ones = jnp.ones((BAND_PAD, dh), BF16)

    def chunks(clip_start):
        for c in range(n_chunks):
            r0 = c * CHUNK
            start = pl.multiple_of((t * n_chunks + c) * CHUNK, CHUNK)
            qc = q_ref[r0:r0 + CHUNK, :]
            q4 = jnp.concatenate([qc[:, g * dh:(g + 1) * dh] for g in range(group)], axis=0)
            kb = k_ref[pl.ds(start, BAND_PAD), :]
            vb = jnp.concatenate([v_ref[pl.ds(start, BAND_PAD), :], ones], axis=1)
            s = lax.dot_general(q4, kb, (((1,), (1,)), ((), ())), preferred_element_type=F32) + bias
            if clip_start:
                first_valid = WINDOW - (chunk0 + t * n_chunks + c) * CHUNK
                col = lax.broadcasted_iota(jnp.int32, (rows_q, BAND_PAD), 1)
                s = jnp.where(col >= first_valid, s, -jnp.inf)
            m = jnp.max(s, axis=-1, keepdims=True)
            p = jnp.exp(s - m).astype(BF16)
            ov = jnp.dot(p, vb, preferred_element_type=F32)
            o4 = ov[:, :dh] / ov[:, dh:]
            for g in range(group):
                o_ref[r0:r0 + CHUNK, g * dh:(g + 1) * dh] = o4[g * CHUNK:(g + 1) * CHUNK].astype(o_ref.dtype)

    starts_early = (chunk0 + t * n_chunks) < LEFT_CHUNKS

    @pl.when(starts_early)
    def _():
        chunks(True)

    @pl.when(jnp.logical_not(starts_early))
    def _():
        chunks(False)


def _band_attention(q, kv_band, bias, layer, *, seq_len, tq, chunk0, name):
    M = q.shape[0]
    B, tp, _ = kv_band.shape
    hkv = bias.shape[1]
    group = bias.shape[2] // CHUNK
    dh = q.shape[1] // (hkv * group)
    nt = seq_len // tq
    body = functools.partial(_band_attention_body, n_chunks=tq // CHUNK, group=group, dh=dh, chunk0=chunk0)
    return pl.pallas_call(
        body,
        grid=(B, hkv, nt),
        in_specs=[
            pl.BlockSpec((tq, group * dh), lambda b, n, t: (b * nt + t, n)),
            pl.BlockSpec((None, tp, dh), lambda b, n, t: (b, 0, n)),
            pl.BlockSpec((None, tp, dh), lambda b, n, t: (b, 0, hkv + n)),
            pl.BlockSpec((None, None, group * CHUNK, BAND_PAD), lambda b, n, t: (layer, n, 0, 0)),
        ],
        out_specs=pl.BlockSpec((tq, group * dh), lambda b, n, t: (b * nt + t, n)),
        out_shape=jax.ShapeDtypeStruct(q.shape, BF16),
        compiler_params=_params("arbitrary", "arbitrary", "arbitrary"),
        name=name,
    )(q, kv_band, kv_band, bias)


def _bias_table(rel_bias, hkv):
    L, H, n_rel = rel_bias.shape
    max_rel = (n_rel - 1) // 2
    u = np.arange(BAND + CHUNK - 1)
    line = rel_bias[:, :, np.clip(WINDOW + CHUNK - 1 - u, -max_rel, max_rel) + max_rel]
    tab = jnp.stack([line[:, :, CHUNK - 1 - i:CHUNK - 1 - i + BAND] for i in range(CHUNK)], axis=2)
    tab = jnp.pad(tab.astype(F32), ((0, 0), (0, 0), (0, 0), (0, BAND_PAD - BAND)), constant_values=MASKED)
    return tab.reshape(L, hkv, (H // hkv) * CHUNK, BAND_PAD)


def _trunk(x, gla_s0, conv_s0, kv_past, wts, *, seq_len):
    M, D = x.shape
    n_seq = M // seq_len
    tm = min(512, M)
    tm_proj = min(1024, M)
    gains = wts["gains"]
    depth = gains.shape[0] // 4
    n_a = gla_s0.shape[0]
    gla_new, conv_new = [], []
    kv = None
    for l in range(depth):
        if l == n_a:
            kv = _norm_matmul(x, wts["kv_norm"], 0, wts["att_w_kv"], 0, tm=tm_proj, out_dtype=F32,
                              name="kv_proj")
            kv3 = kv.reshape(n_seq, seq_len, kv.shape[1]).astype(BF16)
            if kv_past is None:
                band = jnp.pad(kv3, ((0, 0), (WINDOW, BAND_PAD - BAND), (0, 0)))
            else:
                band = jnp.pad(jnp.concatenate([kv_past.astype(BF16), kv3], axis=1),
                               ((0, 0), (0, BAND_PAD - BAND), (0, 0)))
        if l < n_a:
            proj = _norm_matmul(x, gains, 4 * l, wts["gla_w_in"], l, tm=tm_proj, out_dtype=BF16,
                                name="gla_in_proj")
            o, s_new = _gla_scan(proj, wts["gla_w_gate"], wts["gla_b_gate"], wts["gla_head_norm"], gla_s0, l,
                                 seq_len=seq_len, tb=min(8 * CHUNK, seq_len), name="gla_scan")
            gla_new.append(s_new)
            w_o, w_layer = wts["gla_w_o"], l
        else:
            j = l - n_a
            q = _norm_matmul(x, gains, 4 * l, wts["att_w_q"], j, tm=tm_proj, out_dtype=BF16, name="att_q_proj",
                             out_scale=wts["att_scale"])
            o = _band_attention(q, band, wts["att_bias"], j, seq_len=seq_len, tq=min(8 * CHUNK, seq_len),
                                chunk0=0 if kv_past is None else LEFT_CHUNKS, name="band_attention")
            w_o, w_layer = wts["att_w_o"], j
        x = _matmul_postnorm(o, w_o, w_layer, gains, 4 * l + 1, x, tm=tm, name="mixer_out_proj")
        h, tail = _ffn_up(x, gains, l, wts["ffn_w_up"], wts["ffn_conv_wb"], conv_s0, seq_len=seq_len,
                          tm=tm_proj, tf=2 * MXU_DIM, name="ffn_up")
        x = _matmul_postnorm(h, wts["ffn_w_down"], l, gains, 4 * l + 3, x, tm=tm, name="ffn_down")
        conv_new.append(tail[:, 6:8])
    return x, jnp.stack(gla_new), jnp.stack(conv_new), kv


def kernel(x_prompt, x_sample, state_gla, state_ffn_conv, cache_k, cache_v, norm_gains, gla_w_in,
           gla_w_gate, gla_b_gate, gla_head_norm, gla_w_o, kv_norm, att_w_kv, att_w_q, att_rel_bias,
           att_w_o, ffn_w_up, ffn_conv_w, ffn_conv_b, ffn_w_down):
    B, S, D = x_prompt.shape
    DB, DS, _ = x_sample.shape
    n_a, _, H, DK, DV = state_gla.shape
    depth = norm_gains.shape[0]
    hkv, dh = cache_k.shape[2], cache_k.shape[3]
    F = ffn_w_down.shape[1]
    assert cache_k.shape[1] == WINDOW and S % CHUNK == 0 and DS == CHUNK
    qkvr = 2 * H * DK + 2 * H * DV
    rank = gla_w_in.shape[2] - qkvr
    tn_in = 5 * MXU_DIM
    in_pad = -(qkvr + LANES) % tn_in + LANES - rank

    wts = dict(
        gains=norm_gains.reshape(depth * 4, 1, D),
        gla_w_in=_tile_major(jnp.pad(gla_w_in, ((0, 0), (0, 0), (0, in_pad))).astype(BF16), tn_in),
        gla_w_gate=jnp.pad(gla_w_gate, ((0, 0), (0, LANES - rank), (0, 0))),
        gla_b_gate=gla_b_gate[:, None, :],
        gla_head_norm=gla_head_norm[:, None, :],
        gla_w_o=gla_w_o.astype(BF16),
        kv_norm=kv_norm[None, None, :],
        att_w_kv=_tile_major(att_w_kv.astype(BF16)[None], 4 * MXU_DIM),
        att_w_q=att_w_q.astype(BF16)[:, None],
        att_bias=_bias_table(att_rel_bias, hkv),
        att_scale=dh ** -0.5,
        att_w_o=att_w_o.astype(BF16),
        ffn_w_up=ffn_w_up.astype(BF16),
        ffn_conv_wb=jnp.concatenate([ffn_conv_w, ffn_conv_b[:, None, :]], axis=1),
        ffn_w_down=ffn_w_down.astype(BF16),
    )

    y_p, gla_p, conv_p, kv_p = _trunk(
        x_prompt.reshape(B * S, D), jnp.zeros((n_a, B, H, DK, DV), F32),
        jnp.zeros((depth, B, 8, F), F32), None, wts, seq_len=S)
    kv_past = jnp.concatenate([cache_k.reshape(DB, WINDOW, hkv * dh),
                               cache_v.reshape(DB, WINDOW, hkv * dh)], axis=-1)
    state8 = jnp.pad(state_ffn_conv, ((0, 0), (0, 0), (6, 0), (0, 0)))
    y_s, gla_s, conv_s, kv_s = _trunk(
        x_sample.reshape(DB * DS, D), state_gla, state8, kv_past, wts, seq_len=DS)

    keep = min(WINDOW, S)
    kv_p = kv_p.reshape(B, S, 2 * hkv * dh)[:, S - keep:].reshape(B, keep, 2, hkv, dh)
    kv_s = kv_s.reshape(DB, DS, 2, hkv, dh)
    return (y_p.reshape(B, S, D), y_s.reshape(DB, DS, D), gla_p, gla_s, conv_p, conv_s,
            kv_p[:, :, 0], kv_p[:, :, 1], kv_s[:, :, 0], kv_s[:, :, 1])
```

```python
import functools

import numpy as np
import jax
import jax.numpy as jnp
from jax import lax
from jax.experimental import pallas as pl
from jax.experimental.pallas import tpu as pltpu

CHUNK = 64
LEFT_CHUNKS = 8
WINDOW = LEFT_CHUNKS * CHUNK
BAND = WINDOW + CHUNK
LANES = 128
SUBLANES = 8
MXU_DIM = 256
FFN_SLAB = 64
BAND_PAD = ((BAND + LANES - 1) // LANES) * LANES
GLA_TAU = 16.0
EPS = 1e-6
MASKED = -1e30
VMEM_LIMIT = 56 * 1024 * 1024

F32 = jnp.float32
BF16 = jnp.bfloat16


def _params(*sem):
    return pltpu.CompilerParams(dimension_semantics=sem, vmem_limit_bytes=VMEM_LIMIT)


def _rms(x, g):
    ms = jnp.mean(x * x, axis=-1, keepdims=True)
    return x * lax.rsqrt(ms + EPS) * g


def _split_bf16(x, parts):
    out = []
    for _ in range(parts - 1):
        hi = x.astype(BF16)
        out.append(hi)
        x = x - hi.astype(F32)
    out.append(x.astype(BF16))
    return out


def _tile_major(w, tn):
    L, K, N = w.shape
    return w.reshape(L, K, N // tn, tn).transpose(0, 2, 1, 3)


def _norm_matmul_body(x_ref, g_ref, w_ref, o_ref, xn_ref, *, out_scale):
    @pl.when(pl.program_id(1) == 0)
    def _():
        xn_ref[...] = _rms(x_ref[...], g_ref[...]).astype(BF16)

    y = jnp.dot(xn_ref[...], w_ref[...], preferred_element_type=F32)
    if out_scale != 1.0:
        y = y * out_scale
    o_ref[...] = y.astype(o_ref.dtype)


def _norm_matmul(x, gains, gain_row, w, layer, *, tm, out_dtype, name, out_scale=1.0):
    M, D = x.shape
    _, nj, _, tn = w.shape
    return pl.pallas_call(
        functools.partial(_norm_matmul_body, out_scale=out_scale),
        grid=(M // tm, nj),
        in_specs=[
            pl.BlockSpec((tm, D), lambda i, j: (i, 0)),
            pl.BlockSpec((None, 1, D), lambda i, j: (gain_row, 0, 0)),
            pl.BlockSpec((None, None, D, tn), lambda i, j: (layer, j, 0, 0)),
        ],
        out_specs=pl.BlockSpec((tm, tn), lambda i, j: (i, j)),
        out_shape=jax.ShapeDtypeStruct((M, nj * tn), out_dtype),
        scratch_shapes=[pltpu.VMEM((tm, D), BF16)],
        compiler_params=_params("arbitrary", "arbitrary"),
        name=name,
    )(x, gains, w)


def _matmul_postnorm_body(a_ref, w_ref, g_ref, x_ref, o_ref):
    o_ref[...] = jnp.dot(a_ref[...], w_ref[...], preferred_element_type=F32)
    o_ref[...] = x_ref[...] + _rms(o_ref[...], g_ref[...])


def _matmul_postnorm(a, w, layer, gains, gain_row, x, *, tm, name):
    M, K = a.shape
    D = w.shape[2]
    return pl.pallas_call(
        _matmul_postnorm_body,
        grid=(M // tm,),
        in_specs=[
            pl.BlockSpec((tm, K), lambda i: (i, 0)),
            pl.BlockSpec((None, K, D), lambda i: (layer, 0, 0), pipeline_mode=pl.Buffered(1)),
            pl.BlockSpec((None, 1, D), lambda i: (gain_row, 0, 0)),
            pl.BlockSpec((tm, D), lambda i: (i, 0)),
        ],
        out_specs=pl.BlockSpec((tm, D), lambda i: (i, 0)),
        out_shape=jax.ShapeDtypeStruct((M, D), F32),
        compiler_params=_params("arbitrary"),
        name=name,
    )(a, w, gains, x)


def _ffn_up_matmul(xn_ref, wg_ref, wv_ref, gv_ref, slot):
    xn = xn_ref[...]
    gv_ref[slot, 0, SUBLANES:] = jnp.dot(xn, wg_ref[...], preferred_element_type=F32)
    gv_ref[slot, 1, SUBLANES:] = jnp.dot(xn, wv_ref[...], preferred_element_type=F32)


def _ffn_gate_tile(f, slot, gv_ref, cwb_ref, st_ref, tail_ref, carry_ref, h_ref, *, tm, tf, seg):
    if isinstance(f, int):
        cols = slice(f * tf, (f + 1) * tf)
    else:
        cols = pl.ds(pl.multiple_of(f * tf, tf), tf)
    cwb = cwb_ref[:, cols]
    cw_old, cw_mid, cw_new, cb = cwb[0:1], cwb[1:2], cwb[2:3], cwb[3:4]
    g_ref = gv_ref.at[slot, 0]
    v_ref = gv_ref.at[slot, 1]
    lead = SUBLANES
    slab = FFN_SLAB
    g_ref[0:lead] = carry_ref[:, cols] if seg >= tm else st_ref[0, :, cols]
    for r in range(tm // slab):
        r0 = lead + r * slab
        g_new = g_ref[r0:r0 + slab]
        g_mid = g_ref[r0 - 1:r0 - 1 + slab]
        g_old = g_ref[r0 - 2:r0 - 2 + slab]
        if seg < tm and (r * slab) % seg == 0:
            prev = st_ref[(r * slab) // seg, :, cols]
            row = lax.broadcasted_iota(jnp.int32, (slab, tf), 0)
            g_mid = jnp.where(row == 0, prev[7:8], g_mid)
            g_old = jnp.where(row == 0, prev[6:7], jnp.where(row == 1, prev[7:8], g_old))
        if seg < tm and ((r + 1) * slab) % seg == 0:
            tail_ref[(r * slab) // seg, :, cols] = g_ref[r0 + slab - 8:r0 + slab]
        conv = cb + cw_old * g_old + cw_mid * g_mid + cw_new * g_new
        h = conv * jax.nn.sigmoid(conv) * v_ref[r0:r0 + slab]
        h_ref[r * slab:(r + 1) * slab, :] = h.astype(BF16)
    if seg >= tm:
        last = g_ref[tm:tm + lead]
        carry_ref[:, cols] = last
        tail_ref[0, :, cols] = last


def _ffn_up_body(x_ref, g2_ref, wg_ref, wv_ref, cwb_ref, st_ref, h_ref, tail_ref, xn_ref, gv_ref, carry_ref,
                 *, tm, tf, nf, seg, tiles_per_seq):
    i = pl.program_id(0)
    s = pl.program_id(1)
    gate_tile = functools.partial(_ffn_gate_tile, gv_ref=gv_ref, cwb_ref=cwb_ref, st_ref=st_ref,
                                  tail_ref=tail_ref, carry_ref=carry_ref, h_ref=h_ref, tm=tm, tf=tf, seg=seg)

    @pl.when(s == 0)
    def _():
        xn_ref[...] = _rms(x_ref[...], g2_ref[...]).astype(BF16)
        if seg >= tm:
            @pl.when(i % tiles_per_seq == 0)
            def _():
                carry_ref[...] = st_ref[0]

        _ffn_up_matmul(xn_ref, wg_ref, wv_ref, gv_ref, 0)

    for par in range(2):
        @pl.when((s >= 1) & (s < nf) & (s % 2 == par))
        def _():
            gate_tile(s - 1, 1 - par)
            _ffn_up_matmul(xn_ref, wg_ref, wv_ref, gv_ref, par)

    @pl.when(s == nf)
    def _():
        gate_tile(nf - 1, (nf - 1) % 2)


def _ffn_up(x, gains, layer, w_up, conv_wb, state8, *, seq_len, tm, tf, name):
    M, D = x.shape
    F = w_up.shape[2] // 2
    nf = F // tf
    n_seq = M // seq_len
    up = lambda s: jnp.minimum(s, nf - 1)
    done = lambda s: jnp.maximum(s - 1, 0)
    if seq_len >= tm:
        tiles_per_seq = seq_len // tm
        seq_blk = 1
        seq_of = lambda i: i // tiles_per_seq
        n_tail = M // tm
    else:
        tiles_per_seq = 1
        seq_blk = tm // seq_len
        seq_of = lambda i: i
        n_tail = n_seq
    body = functools.partial(_ffn_up_body, tm=tm, tf=tf, nf=nf, seg=seq_len, tiles_per_seq=tiles_per_seq)
    h, tails = pl.pallas_call(
        body,
        grid=(M // tm, nf + 1),
        in_specs=[
            pl.BlockSpec((tm, D), lambda i, s: (i, 0)),
            pl.BlockSpec((None, 1, D), lambda i, s: (4 * layer + 2, 0, 0)),
            pl.BlockSpec((None, D, tf), lambda i, s: (layer, 0, up(s))),
            pl.BlockSpec((None, D, tf), lambda i, s: (layer, 0, nf + up(s))),
            pl.BlockSpec((None, 4, F), lambda i, s: (layer, 0, 0)),
            pl.BlockSpec((None, seq_blk, 8, F), lambda i, s: (layer, seq_of(i), 0, 0)),
        ],
        out_specs=[
            pl.BlockSpec((tm, tf), lambda i, s: (i, done(s))),
            pl.BlockSpec((seq_blk, 8, F), lambda i, s: (i, 0, 0)),
        ],
        out_shape=[
            jax.ShapeDtypeStruct((M, F), BF16),
            jax.ShapeDtypeStruct((n_tail, 8, F), F32),
        ],
        scratch_shapes=[
            pltpu.VMEM((tm, D), BF16),
            pltpu.VMEM((2, 2, SUBLANES + tm, tf), F32),
            pltpu.VMEM((8, F), F32),
        ],
        compiler_params=_params("arbitrary", "arbitrary"),
        name=name,
    )(x, gains, w_up, w_up, conv_wb, state8)
    return h, tails[tiles_per_seq - 1::tiles_per_seq]


def _gla_levels(unit):
    halves = []
    h = unit // 2
    while h >= 1:
        halves.append(h)
        h //= 2
    return [h for h in halves if h >= SUBLANES], [h for h in halves if h < SUBLANES]


def _gla_constants(unit):
    big, small = _gla_levels(unit)
    t = np.arange(unit)[:, None]
    u = np.arange(unit)[None, :]
    blocks = [u <= t]
    masks = [t == u]
    for h in big + small:
        anchor = (t // (2 * h)) * (2 * h) + h - 1
        upper = (t % (2 * h)) >= h
        if h in small:
            blocks.append(np.where(upper, (u > anchor) & (u <= t), (u > t) & (u <= anchor)))
        same_group = (t // (2 * h)) == (u // (2 * h))
        masks.append(same_group & upper & ((u % (2 * h)) < h))
    sums = np.concatenate(blocks, axis=0).astype(np.float32)
    pair_mask = np.stack(masks).astype(np.float32)
    return jnp.asarray(sums, BF16), jnp.asarray(pair_mask, F32)


def _gla_scan_body(q_ref, k_ref, v_ref, r_ref, gl_ref, wg_ref, bg_ref, hn_ref, s0_ref, sums_ref, mask_ref,
                   o_ref, sn_ref, S_ref, *, tb, unit, dk):
    t = pl.program_id(2)

    @pl.when(t == 0)
    def _():
        S_ref[...] = s0_ref[...]

    big, small = _gla_levels(unit)
    wg_hi, wg_lo = _split_bf16(wg_ref[...], 2)
    bg = bg_ref[...]
    hn = hn_ref[...]
    sums2 = jnp.concatenate([sums_ref[...]] * 2, axis=1)
    scale = dk ** -0.5
    row = lax.broadcasted_iota(jnp.int32, (unit, 1), 0)
    contract_last = (((1,), (1,)), ((), ()))

    for un in range(tb // unit):
        rows = slice(un * unit, (un + 1) * unit)
        gl = gl_ref[rows, :]
        logits = (jnp.dot(gl, wg_hi, preferred_element_type=F32)
                  + jnp.dot(gl, wg_lo, preferred_element_type=F32) + bg)
        gk = (jnp.minimum(logits, 0.0) - jnp.log(1.0 + jnp.exp(-jnp.abs(logits)))) * (1.0 / GLA_TAU)
        gk2 = jnp.concatenate(_split_bf16(gk, 2), axis=0)
        sm = jnp.dot(sums2, gk2, preferred_element_type=F32)
        b = sm[0:unit]
        b_end = b[unit - 1:unit]
        qs = q_ref[rows, :].astype(F32) * scale
        k = k_ref[rows, :].astype(F32)
        v = v_ref[rows, :]

        A = lax.dot_general(qs.astype(BF16), k.astype(BF16), contract_last,
                            preferred_element_type=F32) * mask_ref[0]
        for li, h in enumerate(big + small):
            upper = (row % (2 * h)) >= h
            if h in big:
                anchors = [b[g * 2 * h + h - 1:g * 2 * h + h] for g in range(unit // (2 * h))]
                anc = jnp.concatenate([jnp.broadcast_to(a, (2 * h, dk)) for a in anchors], axis=0)
                d = b - anc
                e = jnp.exp(jnp.where(upper, d, -d))
            else:
                m = 1 + small.index(h)
                e = jnp.exp(sm[m * unit:(m + 1) * unit])
            z = (jnp.where(upper, qs, k) * e).astype(BF16)
            a_l = lax.dot_general(z, z, contract_last, preferred_element_type=F32)
            A = A + a_l * mask_ref[1 + li]

        S = S_ref[...]
        o = (jnp.dot(A.astype(BF16), v, preferred_element_type=F32)
             + jnp.dot((qs * jnp.exp(b)).astype(BF16), S.astype(BF16), preferred_element_type=F32))
        k_dec_t = jnp.transpose(k * jnp.exp(b_end - b)).astype(BF16)
        decay_col = jnp.transpose(jnp.exp(b[unit - SUBLANES:]))[:, SUBLANES - 1:]
        S_ref[...] = decay_col * S + jnp.dot(k_dec_t, v, preferred_element_type=F32)
        r = r_ref[rows, :].astype(F32)
        o_ref[rows, :] = (_rms(o, hn) * (r * jax.nn.sigmoid(r))).astype(o_ref.dtype)

    @pl.when(t == pl.num_programs(2) - 1)
    def _():
        sn_ref[...] = S_ref[...]


def _gla_scan(proj, w_gate, b_gate, head_norm, s0, layer, *, seq_len, tb, name):
    _, B, H, DK, DV = s0.shape
    M = proj.shape[0]
    nt = seq_len // tb
    unit = min(LANES, tb)
    v_blk0 = (2 * H * DK) // DV
    gl_blk = (2 * H * DK + 2 * H * DV) // LANES
    sums, pair_mask = _gla_constants(unit)
    body = functools.partial(_gla_scan_body, tb=tb, unit=unit, dk=DK)
    return pl.pallas_call(
        body,
        grid=(B, H, nt),
        in_specs=[
            pl.BlockSpec((tb, DK), lambda b, h, t: (b * nt + t, h)),
            pl.BlockSpec((tb, DK), lambda b, h, t: (b * nt + t, H + h)),
            pl.BlockSpec((tb, DV), lambda b, h, t: (b * nt + t, v_blk0 + h)),
            pl.BlockSpec((tb, DV), lambda b, h, t: (b * nt + t, v_blk0 + H + h)),
            pl.BlockSpec((tb, LANES), lambda b, h, t: (b * nt + t, gl_blk)),
            pl.BlockSpec((None, LANES, DK), lambda b, h, t: (layer, 0, h)),
            pl.BlockSpec((None, 1, DK), lambda b, h, t: (layer, 0, h)),
            pl.BlockSpec((None, 1, DV), lambda b, h, t: (layer, 0, 0)),
            pl.BlockSpec((None, None, None, DK, DV), lambda b, h, t: (layer, b, h, 0, 0)),
            pl.BlockSpec(sums.shape, lambda b, h, t: (0, 0)),
            pl.BlockSpec(pair_mask.shape, lambda b, h, t: (0, 0, 0)),
        ],
        out_specs=[
            pl.BlockSpec((tb, DV), lambda b, h, t: (b * nt + t, h)),
            pl.BlockSpec((None, None, DK, DV), lambda b, h, t: (b, h, 0, 0)),
        ],
        out_shape=[
            jax.ShapeDtypeStruct((M, H * DV), BF16),
            jax.ShapeDtypeStruct((B, H, DK, DV), F32),
        ],
        scratch_shapes=[pltpu.VMEM((DK, DV), F32)],
        compiler_params=_params("arbitrary", "arbitrary", "arbitrary"),
        name=name,
    )(proj, proj, proj, proj, proj, w_gate, b_gate, head_norm, s0, sums, pair_mask)


def _band_attention_body(q_ref, k_ref, v_ref, bias_ref, o_ref, *, n_chunks, group, dh, chunk0):
    t = pl.program_id(2)
    rows_q = group * CHUNK
    bias = bias_ref[...]
    ones = jnp.ones((BAND_PAD, dh), BF16)

    def chunks(clip_start):
        for c in range(n_chunks):
            r0 = c * CHUNK
            start = pl.multiple_of((t * n_chunks + c) * CHUNK, CHUNK)
            qc = q_ref[r0:r0 + CHUNK, :]
            q4 = jnp.concatenate([qc[:, g * dh:(g + 1) * dh] for g in range(group)], axis=0)
            kb = k_ref[pl.ds(start, BAND_PAD), :]
            vb = jnp.concatenate([v_ref[pl.ds(start, BAND_PAD), :], ones], axis=1)
            s = lax.dot_general(q4, kb, (((1,), (1,)), ((), ())), preferred_element_type=F32) + bias
            if clip_start:
                first_valid = WINDOW - (chunk0 + t * n_chunks + c) * CHUNK
                col = lax.broadcasted_iota(jnp.int32, (rows_q, BAND_PAD), 1)
                s = jnp.where(col >= first_valid, s, -jnp.inf)
            m = jnp.max(s, axis=-1, keepdims=True)
            p = jnp.exp(s - m).astype(BF16)
            ov = jnp.dot(p, vb, preferred_element_type=F32)
            o4 = ov[:, :dh] / ov[:, dh:]
            for g in range(group):
                o_ref[r0:r0 + CHUNK, g * dh:(g + 1) * dh] = o4[g * CHUNK:(g + 1) * CHUNK].astype(o_ref.dtype)

    starts_early = (chunk0 + t * n_chunks) < LEFT_CHUNKS

    @pl.when(starts_early)
    def _():
        chunks(True)

    @pl.when(jnp.logical_not(starts_early))
    def _():
        chunks(False)


def _band_attention(q, kv_band, bias, layer, *, seq_len, tq, chunk0, name):
    M = q.shape[0]
    B, tp, _ = kv_band.shape
    hkv = bias.shape[1]
    group = bias.shape[2] // CHUNK
    dh = q.shape[1] // (hkv * group)
    nt = seq_len // tq
    body = functools.partial(_band_attention_body, n_chunks=tq // CHUNK, group=group, dh=dh, chunk0=chunk0)
    return pl.pallas_call(
        body,
        grid=(B, hkv, nt),
        in_specs=[
            pl.BlockSpec((tq, group * dh), lambda b, n, t: (b * nt + t, n)),
            pl.BlockSpec((None, tp, dh), lambda b, n, t: (b, 0, n)),
            pl.BlockSpec((None, tp, dh), lambda b, n, t: (b, 0, hkv + n)),
            pl.BlockSpec((None, None, group * CHUNK, BAND_PAD), lambda b, n, t: (layer, n, 0, 0)),
        ],
        out_specs=pl.BlockSpec((tq, group * dh), lambda b, n, t: (b * nt + t, n)),
        out_shape=jax.ShapeDtypeStruct(q.shape, BF16),
        compiler_params=_params("arbitrary", "arbitrary", "arbitrary"),
        name=name,
    )(q, kv_band, kv_band, bias)


def _bias_table(rel_bias, hkv):
    L, H, n_rel = rel_bias.shape
    max_rel = (n_rel - 1) // 2
    u = np.arange(BAND + CHUNK - 1)
    line = rel_bias[:, :, np.clip(WINDOW + CHUNK - 1 - u, -max_rel, max_rel) + max_rel]
    tab = jnp.stack([line[:, :, CHUNK - 1 - i:CHUNK - 1 - i + BAND] for i in range(CHUNK)], axis=2)
    tab = jnp.pad(tab.astype(F32), ((0, 0), (0, 0), (0, 0), (0, BAND_PAD - BAND)), constant_values=MASKED)
    return tab.reshape(L, hkv, (H // hkv) * CHUNK, BAND_PAD)


def _trunk(x, gla_s0, conv_s0, kv_past, wts, *, seq_len):
    M, D = x.shape
    n_seq = M // seq_len
    tm = min(512, M)
    tm_proj = min(1024, M)
    gains = wts["gains"]
    depth = gains.shape[0] // 4
    n_a = gla_s0.shape[0]
    gla_new, conv_new = [], []
    kv = None
    for l in range(depth):
        if l == n_a:
            kv = _norm_matmul(x, wts["kv_norm"], 0, wts["att_w_kv"], 0, tm=tm_proj, out_dtype=F32,
                              name="kv_proj")
            kv3 = kv.reshape(n_seq, seq_len, kv.shape[1]).astype(BF16)
            if kv_past is None:
                band = jnp.pad(kv3, ((0, 0), (WINDOW, BAND_PAD - BAND), (0, 0)))
            else:
                band = jnp.pad(jnp.concatenate([kv_past.astype(BF16), kv3], axis=1),
                               ((0, 0), (0, BAND_PAD - BAND), (0, 0)))
        if l < n_a:
            proj = _norm_matmul(x, gains, 4 * l, wts["gla_w_in"], l, tm=tm_proj, out_dtype=BF16,
                                name="gla_in_proj")
            o, s_new = _gla_scan(proj, wts["gla_w_gate"], wts["gla_b_gate"], wts["gla_head_norm"], gla_s0, l,
                                 seq_len=seq_len, tb=min(16 * CHUNK, seq_len), name="gla_scan")
            gla_new.append(s_new)
            w_o, w_layer = wts["gla_w_o"], l
        else:
            j = l - n_a
            q = _norm_matmul(x, gains, 4 * l, wts["att_w_q"], j, tm=tm_proj, out_dtype=BF16, name="att_q_proj",
                             out_scale=wts["att_scale"])
            o = _band_attention(q, band, wts["att_bias"], j, seq_len=seq_len, tq=min(16 * CHUNK, seq_len),
                                chunk0=0 if kv_past is None else LEFT_CHUNKS, name="band_attention")
            w_o, w_layer = wts["att_w_o"], j
        x = _matmul_postnorm(o, w_o, w_layer, gains, 4 * l + 1, x, tm=tm, name="mixer_out_proj")
        h, tail = _ffn_up(x, gains, l, wts["ffn_w_up"], wts["ffn_conv_wb"], conv_s0, seq_len=seq_len,
                          tm=tm_proj, tf=2 * MXU_DIM, name="ffn_up")
        x = _matmul_postnorm(h, wts["ffn_w_down"], l, gains, 4 * l + 3, x, tm=tm, name="ffn_down")
        conv_new.append(tail[:, 6:8])
    return x, jnp.stack(gla_new), jnp.stack(conv_new), kv


def kernel(x_prompt, x_sample, state_gla, state_ffn_conv, cache_k, cache_v, norm_gains, gla_w_in,
           gla_w_gate, gla_b_gate, gla_head_norm, gla_w_o, kv_norm, att_w_kv, att_w_q, att_rel_bias,
           att_w_o, ffn_w_up, ffn_conv_w, ffn_conv_b, ffn_w_down):
    B, S, D = x_prompt.shape
    DB, DS, _ = x_sample.shape
    n_a, _, H, DK, DV = state_gla.shape
    depth = norm_gains.shape[0]
    hkv, dh = cache_k.shape[2], cache_k.shape[3]
    F = ffn_w_down.shape[1]
    assert cache_k.shape[1] == WINDOW and S % CHUNK == 0 and DS == CHUNK
    qkvr = 2 * H * DK + 2 * H * DV
    rank = gla_w_in.shape[2] - qkvr
    tn_in = 5 * MXU_DIM
    in_pad = -(qkvr + LANES) % tn_in + LANES - rank

    wts = dict(
        gains=norm_gains.reshape(depth * 4, 1, D),
        gla_w_in=_tile_major(jnp.pad(gla_w_in, ((0, 0), (0, 0), (0, in_pad))).astype(BF16), tn_in),
        gla_w_gate=jnp.pad(gla_w_gate, ((0, 0), (0, LANES - rank), (0, 0))),
        gla_b_gate=gla_b_gate[:, None, :],
        gla_head_norm=gla_head_norm[:, None, :],
        gla_w_o=gla_w_o.astype(BF16),
        kv_norm=kv_norm[None, None, :],
        att_w_kv=_tile_major(att_w_kv.astype(BF16)[None], 4 * MXU_DIM),
        att_w_q=att_w_q.astype(BF16)[:, None],
        att_bias=_bias_table(att_rel_bias, hkv),
        att_scale=dh ** -0.5,
        att_w_o=att_w_o.astype(BF16),
        ffn_w_up=ffn_w_up.astype(BF16),
        ffn_conv_wb=jnp.concatenate([ffn_conv_w, ffn_conv_b[:, None, :]], axis=1),
        ffn_w_down=ffn_w_down.astype(BF16),
    )

    y_p, gla_p, conv_p, kv_p = _trunk(
        x_prompt.reshape(B * S, D), jnp.zeros((n_a, B, H, DK, DV), F32),
        jnp.zeros((depth, B, 8, F), F32), None, wts, seq_len=S)
    kv_past = jnp.concatenate([cache_k.reshape(DB, WINDOW, hkv * dh),
                               cache_v.reshape(DB, WINDOW, hkv * dh)], axis=-1)
    state8 = jnp.pad(state_ffn_conv, ((0, 0), (0, 0), (6, 0), (0, 0)))
    y_s, gla_s, conv_s, kv_s = _trunk(
        x_sample.reshape(DB * DS, D), state_gla, state8, kv_past, wts, seq_len=DS)

    keep = min(WINDOW, S)
    kv_p = kv_p.reshape(B, S, 2 * hkv * dh)[:, S - keep:].reshape(B, keep, 2, hkv, dh)
    kv_s = kv_s.reshape(DB, DS, 2, hkv, dh)
    return (y_p.reshape(B, S, D), y_s.reshape(DB, DS, D), gla_p, gla_s, conv_p, conv_s,
            kv_p[:, :, 0], kv_p[:, :, 1], kv_s[:, :, 0], kv_s[:, :, 1])
```
